```python
import math
import jax
import jax.numpy as jnp
from jax import lax
import numpy as np

D_MODEL = 1024
BATCH = 16
SEQ = 4096
DEPTH = 4

EPS = 1e-6
S5_WIDTH = D_MODEL
S5_GROUP = 16
S5_GROUPS = S5_WIDTH // S5_GROUP
S5_STATE = 64
S5_CHUNK = 128
M_WIDTH = D_MODEL
M_HEADS = 4
M_HEAD_DIM = M_WIDTH // M_HEADS
M_CONV = 4
M_CHUNK = 128
A_HEAD_DIM = 64
A_HEADS = D_MODEL // A_HEAD_DIM
A_KV_HEADS = A_HEADS // 4
A_WINDOW = 128
ROPE_THETA = 500000.0
ROPE_DIM = A_HEAD_DIM // 4
EVEN_IN = 2 * S5_WIDTH + 2 * M_WIDTH
EVEN_OUT = S5_WIDTH + M_WIDTH
A_Q = A_HEADS * A_HEAD_DIM
A_KV = A_KV_HEADS * A_HEAD_DIM
ODD_IN = 2 * A_Q + 2 * A_KV
N_EVEN = (DEPTH + 1) // 2
N_ODD = DEPTH // 2

kernel_name = 'hybrid_s5_mlstm_swa_trunk'

F32 = jnp.float32


def rmsnorm(x, g):
    xf = x.astype(F32)
    return xf * lax.rsqrt(jnp.mean(xf * xf, axis=-1, keepdims=True) + EPS) * g.astype(F32)


def adaln(c, w, b):
    m = jax.nn.silu(c.astype(F32)) @ w.astype(F32) + b.astype(F32)
    shift, scale, gate = jnp.split(m[:, None, :], 3, axis=-1)
    return shift, scale, gate


def _lin_rec(e1, e2):
    a1, b1 = e1
    a2, b2 = e2
    return a1 * a2, a2 * b1 + b2


def s5_mixer(u, a_re, a_im, log_dt, b_re, b_im, c_re, c_im, d_skip, w_glu):
    bsz, seq, _ = u.shape
    n_chunks = seq // S5_CHUNK
    a = lax.complex(a_re.astype(F32), a_im.astype(F32))
    dt = jnp.exp(log_dt.astype(F32))[:, None]
    a_bar = jnp.exp(a * dt)
    b_bar = ((a_bar - 1.0) / a)[..., None] * lax.complex(b_re.astype(F32), b_im.astype(F32))
    c_mat = lax.complex(c_re.astype(F32), c_im.astype(F32))
    u_chunks = jnp.moveaxis(u.reshape(bsz, n_chunks, S5_CHUNK, S5_GROUPS, S5_GROUP), 1, 0)

    def chunk_step(state, u_c):
        bu = jnp.einsum('btgp,gnp->btgn', u_c.astype(jnp.complex64), b_bar)
        a_cum, xs = lax.associative_scan(_lin_rec, (jnp.broadcast_to(a_bar, bu.shape), bu), axis=1)
        xs = xs + a_cum * state[:, None]
        y = jnp.einsum('btgn,gpn->btgp', xs, c_mat).real
        return xs[:, -1], y

    state0 = jnp.zeros((bsz, S5_GROUPS, S5_STATE), jnp.complex64)
    _, y = lax.scan(chunk_step, state0, u_chunks)
    y = jnp.moveaxis(y, 0, 1).reshape(bsz, seq, S5_WIDTH)
    y = jax.nn.gelu(y + d_skip.astype(F32) * u)
    return y * jax.nn.sigmoid(y @ w_glu.astype(F32))


def causal_conv(x, w, b):
    seq = x.shape[1]
    k_taps = w.shape[0]
    xp = jnp.pad(x, ((0, 0), (k_taps - 1, 0), (0, 0)))
    out = xp[:, 0:seq] * w[0]
    for t in range(1, k_taps):
        out = out + xp[:, t:t + seq] * w[t]
    return out + b


def mlstm_cell(q, k, v, log_i, log_f):
    bsz, nh, seq, dh = q.shape
    n_chunks = seq // M_CHUNK
    chunk = lambda t: jnp.moveaxis(t.reshape(bsz, nh, n_chunks, M_CHUNK, *t.shape[3:]), 2, 0)
    tri = jnp.tril(jnp.ones((M_CHUNK, M_CHUNK), dtype=bool))

    def step(carry, inp):
        c_st, n_st, m_st = carry
        qc, kc, vc, li, lf = inp
        b = jnp.cumsum(lf, axis=-1)
        d_log = b[..., :, None] - b[..., None, :] + li[..., None, :]
        d_log = jnp.where(tri, d_log, -jnp.inf)
        inter = b + m_st[..., None]
        m_t = jnp.maximum(inter, jnp.max(d_log, axis=-1))
        s = jnp.einsum('bhtd,bhsd->bhts', qc, kc) * jnp.exp(d_log - m_t[..., None])
        g = jnp.exp(inter - m_t)
        num = jnp.einsum('bhts,bhsd->bhtd', s, vc) + g[..., None] * jnp.einsum('bhtd,bhde->bhte', qc, c_st)
        den = jnp.sum(s, axis=-1) + g * jnp.einsum('bhtd,bhd->bht', qc, n_st)
        h = num / jnp.maximum(jnp.abs(den), jnp.exp(-m_t))[..., None]
        b_last = b[..., -1]
        w_log = b_last[..., None] - b + li
        m_new = jnp.maximum(b_last + m_st, jnp.max(w_log, axis=-1))
        w_k = jnp.exp(w_log - m_new[..., None])
        decay = jnp.exp(b_last + m_st - m_new)
        c_new = decay[..., None, None] * c_st + jnp.einsum('bhs,bhsd,bhse->bhde', w_k, kc, vc)
        n_new = decay[..., None] * n_st + jnp.einsum('bhs,bhsd->bhd', w_k, kc)
        return (c_new, n_new, m_new), h

    carry0 = (jnp.zeros((bsz, nh, dh, dh), F32), jnp.zeros((bsz, nh, dh), F32), jnp.zeros((bsz, nh), F32))
    _, h = lax.scan(step, carry0, (chunk(q), chunk(k), chunk(v), chunk(log_i), chunk(log_f)))
    return jnp.moveaxis(h, 0, 2).reshape(bsz, nh, seq, dh)


def mlstm_mixer(m_in, conv_w, conv_b, wq, wk, wv, wo, w_if, b_if, norm_g, skip):
    bsz, seq, _ = m_in.shape
    xc = jax.nn.silu(causal_conv(m_in, conv_w.astype(F32), conv_b.astype(F32)))
    heads = lambda t: t.reshape(bsz, seq, M_HEADS, M_HEAD_DIM)
    q = jnp.einsum('blhd,hde->bhle', heads(xc), wq.astype(F32))
    k = jnp.einsum('blhd,hde->bhle', heads(xc), wk.astype(F32)) * (M_HEAD_DIM ** -0.5)
    v = jnp.einsum('blhd,hde->bhle', heads(m_in), wv.astype(F32))
    o = jax.nn.sigmoid(jnp.einsum('blhd,hde->blhe', heads(m_in), wo.astype(F32)))
    gates = jnp.concatenate([xc, m_in], axis=-1) @ w_if.astype(F32) + b_if.astype(F32)
    gates = jnp.moveaxis(gates, -1, 1)
    log_i = gates[:, :M_HEADS]
    log_f = jax.nn.log_sigmoid(gates[:, M_HEADS:])
    h = jnp.moveaxis(mlstm_cell(q, k, v, log_i, log_f), 1, 2)
    h = rmsnorm(o * h, norm_g) + skip.astype(F32).reshape(M_HEADS, M_HEAD_DIM) * heads(xc)
    return h.reshape(bsz, seq, M_WIDTH)


def s5_mlstm_layer(hn, w_in, a_re, a_im, log_dt, b_re, b_im, c_re, c_im, d_skip, w_glu,
                   conv_w, conv_b, wq, wk, wv, wo, w_if, b_if, m_norm_g, m_skip, w_out):
    proj = hn @ w_in.astype(F32)
    s5_u, s5_z, m_in, m_z = jnp.split(proj, [S5_WIDTH, 2 * S5_WIDTH, 2 * S5_WIDTH + M_WIDTH], axis=-1)
    y_s5 = s5_mixer(s5_u, a_re, a_im, log_dt, b_re, b_im, c_re, c_im, d_skip, w_glu) * jax.nn.silu(s5_z)
    y_m = mlstm_mixer(m_in, conv_w, conv_b, wq, wk, wv, wo, w_if, b_if, m_norm_g, m_skip) * jax.nn.silu(m_z)
    return jnp.concatenate([y_s5, y_m], axis=-1) @ w_out.astype(F32)


def rope_partial(x, positions):
    half = ROPE_DIM // 2
    inv_freq = jnp.exp(-math.log(ROPE_THETA) * jnp.arange(half, dtype=F32) / half)
    ang = positions.astype(F32)[..., None] * inv_freq
    cos = jnp.cos(ang)[:, :, None, :]
    sin = jnp.sin(ang)[:, :, None, :]
    x1 = x[..., :half]
    x2 = x[..., half:ROPE_DIM]
    return jnp.concatenate([x1 * cos - x2 * sin, x2 * cos + x1 * sin, x[..., ROPE_DIM:]], axis=-1)


def sliding_window_attention(q, k, v, sinks):
    bsz, seq, _, hd = q.shape
    grp = A_HEADS // A_KV_HEADS
    n_blocks = seq // A_WINDOW
    qb = q.reshape(bsz, n_blocks, A_WINDOW, A_KV_HEADS, grp, hd)
    kp = jnp.pad(k, ((0, 0), (A_WINDOW, 0), (0, 0), (0, 0)))
    vp = jnp.pad(v, ((0, 0), (A_WINDOW, 0), (0, 0), (0, 0)))
    qi = jnp.arange(A_WINDOW)[:, None] + A_WINDOW
    ki = jnp.arange(2 * A_WINDOW)[None, :]
    rel = qi - ki
    band = (rel >= 0) & (rel < A_WINDOW)
    sink_logits = sinks.astype(F32).reshape(A_KV_HEADS, grp)[None, :, :, None, None]
    scale = hd ** -0.5

    def block(j):
        qj = lax.dynamic_index_in_dim(qb, j, axis=1, keepdims=False)
        kj = lax.dynamic_slice_in_dim(kp, j * A_WINDOW, 2 * A_WINDOW, axis=1)
        vj = lax.dynamic_slice_in_dim(vp, j * A_WINDOW, 2 * A_WINDOW, axis=1)
        s = jnp.einsum('bqkgd,bskd->bkgqs', qj, kj) * scale
        valid = band & (ki + j * A_WINDOW - A_WINDOW >= 0)
        s = jnp.where(valid, s, -jnp.inf)
        sink = jnp.broadcast_to(sink_logits, s.shape[:-1] + (1,))
        p = jax.nn.softmax(jnp.concatenate([s, sink], axis=-1), axis=-1)[..., :-1]
        return jnp.einsum('bkgqs,bskd->bqkgd', p, vj)

    out = lax.map(block, jnp.arange(n_blocks))
    return jnp.moveaxis(out, 0, 1).reshape(bsz, seq, A_Q)


def swa_layer(hn, positions, w_in, q_g, k_g, sinks, w_out):
    bsz, seq, _ = hn.shape
    proj = hn @ w_in.astype(F32)
    q, k, v, z = jnp.split(proj, [A_Q, A_Q + A_KV, A_Q + 2 * A_KV], axis=-1)
    q = rope_partial(rmsnorm(q.reshape(bsz, seq, A_HEADS, A_HEAD_DIM), q_g), positions)
    k = rope_partial(rmsnorm(k.reshape(bsz, seq, A_KV_HEADS, A_HEAD_DIM), k_g), positions)
    v = v.reshape(bsz, seq, A_KV_HEADS, A_HEAD_DIM)
    a = sliding_window_attention(q, k, v, sinks)
    return (a * jax.nn.silu(z)) @ w_out.astype(F32)


def setup_inputs(seed: int = 0) -> dict:
    key = jax.random.key(seed)
    ks = iter(jax.random.split(key, 64))

    def nrm(shape, scale):
        return jax.random.normal(next(ks), shape, F32) * scale

    ne, no = N_EVEN, N_ODD
    g, n, p = S5_GROUPS, S5_STATE, S5_GROUP
    x = nrm((BATCH, SEQ, D_MODEL), 1.0)
    c = nrm((BATCH, D_MODEL), 1.0)
    positions = jnp.arange(SEQ, dtype=jnp.int32)[None, :] + jax.random.randint(next(ks), (BATCH, 1), 0, 1024, dtype=jnp.int32)
    ada_w = nrm((DEPTH, D_MODEL, 3 * D_MODEL), 0.5 * D_MODEL ** -0.5)
    ada_b = nrm((DEPTH, 3 * D_MODEL), 0.02)
    norm_g = 1.0 + nrm((DEPTH, D_MODEL), 0.02)
    ev_w_in = nrm((ne, D_MODEL, EVEN_IN), D_MODEL ** -0.5)
    s5_a_re = -0.5 * jnp.exp(nrm((ne, g, n), 0.05))
    s5_a_im = math.pi * jnp.arange(n, dtype=F32) + nrm((ne, g, n), 0.01)
    s5_log_dt = jax.random.uniform(next(ks), (ne, g), F32, math.log(1e-3), math.log(1e-1))
    s5_b_re = nrm((ne, g, n, p), (2 * p) ** -0.5)
    s5_b_im = nrm((ne, g, n, p), (2 * p) ** -0.5)
    s5_c_re = nrm((ne, g, p, n), (2 * n) ** -0.5)
    s5_c_im = nrm((ne, g, p, n), (2 * n) ** -0.5)
    s5_d = nrm((ne, S5_WIDTH), 1.0)
    s5_w_glu = nrm((ne, S5_WIDTH, S5_WIDTH), S5_WIDTH ** -0.5)
    m_conv_w = nrm((ne, M_CONV, M_WIDTH), M_CONV ** -0.5)
    m_conv_b = nrm((ne, M_WIDTH), 0.02)
    m_wq = nrm((ne, M_HEADS, M_HEAD_DIM, M_HEAD_DIM), M_HEAD_DIM ** -0.5)
    m_wk = nrm((ne, M_HEADS, M_HEAD_DIM, M_HEAD_DIM), M_HEAD_DIM ** -0.5)
    m_wv = nrm((ne, M_HEADS, M_HEAD_DIM, M_HEAD_DIM), M_HEAD_DIM ** -0.5)
    m_wo = nrm((ne, M_HEADS, M_HEAD_DIM, M_HEAD_DIM), M_HEAD_DIM ** -0.5)
    m_w_if = nrm((ne, 2 * M_WIDTH, 2 * M_HEADS), 0.1 * (2 * M_WIDTH) ** -0.5)
    m_b_if = jnp.concatenate([nrm((ne, M_HEADS), 0.1), jnp.linspace(3.0, 6.0, M_HEADS, dtype=F32) + nrm((ne, M_HEADS), 0.01)], axis=-1)
    m_norm_g = 1.0 + nrm((ne, M_HEADS, M_HEAD_DIM), 0.02)
    m_skip = 1.0 + nrm((ne, M_WIDTH), 0.02)
    ev_w_out = nrm((ne, EVEN_OUT, D_MODEL), EVEN_OUT ** -0.5)
    od_w_in = nrm((no, D_MODEL, ODD_IN), D_MODEL ** -0.5)
    od_q_norm_g = 1.0 + nrm((no, A_HEAD_DIM), 0.02)
    od_k_norm_g = 1.0 + nrm((no, A_HEAD_DIM), 0.02)
    od_sinks = nrm((no, A_HEADS), 1.0)
    od_w_out = nrm((no, A_Q, D_MODEL), A_Q ** -0.5)
    return {'x': x, 'c': c, 'positions': positions, 'ada_w': ada_w, 'ada_b': ada_b, 'norm_g': norm_g,
            'ev_w_in': ev_w_in, 's5_a_re': s5_a_re, 's5_a_im': s5_a_im, 's5_log_dt': s5_log_dt,
            's5_b_re': s5_b_re, 's5_b_im': s5_b_im, 's5_c_re': s5_c_re, 's5_c_im': s5_c_im,
            's5_d': s5_d, 's5_w_glu': s5_w_glu, 'm_conv_w': m_conv_w, 'm_conv_b': m_conv_b,
            'm_wq': m_wq, 'm_wk': m_wk, 'm_wv': m_wv, 'm_wo': m_wo, 'm_w_if': m_w_if, 'm_b_if': m_b_if,
            'm_norm_g': m_norm_g, 'm_skip': m_skip, 'ev_w_out': ev_w_out, 'od_w_in': od_w_in,
            'od_q_norm_g': od_q_norm_g, 'od_k_norm_g': od_k_norm_g, 'od_sinks': od_sinks, 'od_w_out': od_w_out}


def reference(x, c, positions, ada_w, ada_b, norm_g, ev_w_in, s5_a_re, s5_a_im, s5_log_dt,
              s5_b_re, s5_b_im, s5_c_re, s5_c_im, s5_d, s5_w_glu, m_conv_w, m_conv_b,
              m_wq, m_wk, m_wv, m_wo, m_w_if, m_b_if, m_norm_g, m_skip, ev_w_out, od_w_in,
              od_q_norm_g, od_k_norm_g, od_sinks, od_w_out):
    h = x.astype(F32)
    for layer in range(DEPTH):
        shift, scale, gate = adaln(c, ada_w[layer], ada_b[layer])
        hn = rmsnorm(h, norm_g[layer]) * (1.0 + scale) + shift
        if layer % 2 == 0:
            e = layer // 2
            y = s5_mlstm_layer(hn, ev_w_in[e], s5_a_re[e], s5_a_im[e], s5_log_dt[e], s5_b_re[e], s5_b_im[e],
                               s5_c_re[e], s5_c_im[e], s5_d[e], s5_w_glu[e], m_conv_w[e], m_conv_b[e],
                               m_wq[e], m_wk[e], m_wv[e], m_wo[e], m_w_if[e], m_b_if[e], m_norm_g[e],
                               m_skip[e], ev_w_out[e])
        else:
            o = layer // 2
            y = swa_layer(hn, positions, od_w_in[o], od_q_norm_g[o], od_k_norm_g[o], od_sinks[o], od_w_out[o])
        h = h + gate * y
    return h.astype(x.dtype)
```

```python
import functools
import math

import jax
import jax.numpy as jnp
from jax import lax
from jax.experimental import pallas as pl
from jax.experimental.pallas import tpu as pltpu

F32 = jnp.float32
BF16 = jnp.bfloat16
HIGHEST = lax.Precision.HIGHEST

EPS = 1e-6
D_MODEL = 1024
DEPTH = 4
S5_GROUP = 16
S5_STATE = 64
S5_GROUPS = D_MODEL // S5_GROUP
S5_GROUPS_PER_TILE = 16
S5_TILES = S5_GROUPS // S5_GROUPS_PER_TILE
S5_TILE_CH = S5_GROUPS_PER_TILE * S5_GROUP
S5_TILE_ST = S5_GROUPS_PER_TILE * S5_STATE
S5_SCAN_COLS = 256
M_HEADS = 4
M_HEAD_DIM = D_MODEL // M_HEADS
M_CONV = 4
CHUNK = 128
CONV_TAIL = 8
A_HEAD_DIM = 64
A_HEADS = D_MODEL // A_HEAD_DIM
A_KV_HEADS = A_HEADS // 4
A_KV = A_KV_HEADS * A_HEAD_DIM
ROPE_THETA = 500000.0
ROPE_DIM = A_HEAD_DIM // 4
ROPE_HALF = ROPE_DIM // 2
LANES = 128
NEG_BIG = -1e30

TIME_TILE = 32
VMEM_LIMIT = 56 * 1024 * 1024


def _silu(x):
    return x * jax.nn.sigmoid(x)


def _gelu_tanh(x):
    return 0.5 * x * (1.0 + jnp.tanh(math.sqrt(2.0 / math.pi) * (x + 0.044715 * (x * x * x))))


def _params(*sem):
    return pltpu.CompilerParams(dimension_semantics=sem, vmem_limit_bytes=VMEM_LIMIT)


def _adaln_kernel(c_ref, w_ref, b_ref, o_ref):
    s = _silu(c_ref[...])
    o_ref[...] = jnp.dot(s, w_ref[...], preferred_element_type=F32, precision=HIGHEST) + b_ref[...]


def _adaln(c, ada_w, ada_b):
    bsz = c.shape[0]
    return pl.pallas_call(
        _adaln_kernel,
        grid=(DEPTH, 3),
        in_specs=[
            pl.BlockSpec((bsz, D_MODEL), lambda l, j: (0, 0)),
            pl.BlockSpec((None, D_MODEL, D_MODEL), lambda l, j: (l, 0, j)),
            pl.BlockSpec((None, None, 1, D_MODEL), lambda l, j: (l, j, 0, 0)),
        ],
        out_specs=pl.BlockSpec((None, None, bsz, D_MODEL), lambda l, j: (l, j, 0, 0)),
        out_shape=jax.ShapeDtypeStruct((DEPTH, 3, bsz, D_MODEL), F32),
        compiler_params=_params("parallel", "parallel"),
        name="adaln",
    )(c.astype(F32), ada_w.astype(F32), ada_b.astype(F32).reshape(DEPTH, 3, 1, D_MODEL))


def _inproj_kernel(h_ref, mod_ref, g_ref, w_ref, *o_refs):
    x = h_ref[...]
    tt, bsz, d = x.shape
    ms = jnp.mean(x * x, axis=-1, keepdims=True)
    hn = x * lax.rsqrt(ms + EPS) * g_ref[...] * (1.0 + mod_ref[1]) + mod_ref[0]
    hb = hn.reshape(tt * bsz, d).astype(BF16)
    c0 = 0
    for o_ref in o_refs:
        width = o_ref.shape[-1]
        y = jnp.dot(hb, w_ref[:, c0:c0 + width], preferred_element_type=F32)
        o_ref[...] = y.astype(o_ref.dtype).reshape(tt, bsz, width)
        c0 += width


def _inproj(h, mod, layer, g, w, widths):
    seq, bsz, d = h.shape
    n_out = w.shape[1]
    assert sum(widths) == n_out
    tt = min(TIME_TILE, seq)
    return pl.pallas_call(
        _inproj_kernel,
        grid=(seq // tt,),
        in_specs=[
            pl.BlockSpec((tt, bsz, d), lambda i: (i, 0, 0)),
            pl.BlockSpec((None, 3, bsz, d), lambda i: (layer, 0, 0, 0)),
            pl.BlockSpec((1, d), lambda i: (0, 0)),
            pl.BlockSpec((d, n_out), lambda i: (0, 0)),
        ],
        out_specs=[pl.BlockSpec((tt, bsz, wd), lambda i: (i, 0, 0)) for wd in widths],
        out_shape=[jax.ShapeDtypeStruct((seq, bsz, wd), BF16) for wd in widths],
        compiler_params=_params("parallel"),
        name="inproj",
    )(h, mod, g.astype(F32).reshape(1, d), w.astype(BF16))


def _outproj_kernel(*refs, n_parts):
    h_ref, mod_ref = refs[0], refs[1]
    y_refs = refs[2:2 + n_parts]
    w_ref, o_ref = refs[2 + n_parts], refs[3 + n_parts]
    tt, bsz, d = h_ref.shape
    acc = None
    for i, y_ref in enumerate(y_refs):
        k = y_ref.shape[-1]
        part = jnp.dot(y_ref[...].reshape(tt * bsz, k), w_ref[i * k:(i + 1) * k, :],
                       preferred_element_type=F32)
        acc = part if acc is None else acc + part
    o_ref[...] = h_ref[...] + mod_ref[2] * acc.reshape(tt, bsz, d)


def _outproj(h, mod, layer, parts, w):
    seq, bsz, d = h.shape
    tt = min(TIME_TILE, seq)
    n_parts = len(parts)
    act_spec = lambda width: pl.BlockSpec((tt, bsz, width), lambda i: (i, 0, 0))
    return pl.pallas_call(
        functools.partial(_outproj_kernel, n_parts=n_parts),
        grid=(seq // tt,),
        in_specs=[act_spec(d), pl.BlockSpec((None, 3, bsz, d), lambda i: (layer, 0, 0, 0))]
        + [act_spec(p.shape[-1]) for p in parts]
        + [pl.BlockSpec(w.shape, lambda i: (0, 0))],
        out_specs=act_spec(d),
        out_shape=jax.ShapeDtypeStruct((seq, bsz, d), F32),
        compiler_params=_params("parallel"),
        name="outproj",
    )(h, mod, *parts, w.astype(BF16))


def _s5_kernel(u_ref, z_ref, bmat_ref, cmat_ref, are_ref, aim_ref, d_ref, wglu_ref, o_ref,
               st_re, st_im, x_re, x_im, y_acc):
    tt, bsz, d = u_ref.shape
    rows = tt * bsz

    @pl.when(pl.program_id(0) == 0)
    def _():
        st_re[...] = jnp.zeros_like(st_re)
        st_im[...] = jnp.zeros_like(st_im)

    u = u_ref[...].reshape(rows, d)
    for j in range(S5_TILES):
        bu = jnp.dot(u[:, j * S5_TILE_CH:(j + 1) * S5_TILE_CH], bmat_ref[j],
                     preferred_element_type=F32)
        x_re[...] = bu[:, :S5_TILE_ST]
        x_im[...] = bu[:, S5_TILE_ST:]
        for c0 in range(0, S5_TILE_ST, S5_SCAN_COLS):
            cols = slice(c0, c0 + S5_SCAN_COLS)
            st_cols = slice(j * S5_TILE_ST + c0, j * S5_TILE_ST + c0 + S5_SCAN_COLS)
            a_re = jnp.broadcast_to(are_ref[j, :, cols], (bsz, S5_SCAN_COLS))
            a_im = jnp.broadcast_to(aim_ref[j, :, cols], (bsz, S5_SCAN_COLS))

            def step(t, carry, cols=cols, a_re=a_re, a_im=a_im):
                s_re, s_im = carry
                r0 = pl.multiple_of(t * bsz, bsz)
                n_re = a_re * s_re - a_im * s_im + x_re[pl.ds(r0, bsz), cols]
                n_im = a_re * s_im + a_im * s_re + x_im[pl.ds(r0, bsz), cols]
                x_re[pl.ds(r0, bsz), cols] = n_re
                x_im[pl.ds(r0, bsz), cols] = n_im
                return n_re, n_im

            s_re, s_im = lax.fori_loop(0, tt, step, (st_re[:, st_cols], st_im[:, st_cols]))
            st_re[:, st_cols] = s_re
            st_im[:, st_cols] = s_im
        y_acc[:, j * S5_TILE_CH:(j + 1) * S5_TILE_CH] = (
            jnp.dot(x_re[...].astype(BF16), cmat_ref[j, :S5_TILE_ST, :], preferred_element_type=F32)
            + jnp.dot(x_im[...].astype(BF16), cmat_ref[j, S5_TILE_ST:, :], preferred_element_type=F32))

    y = _gelu_tanh(y_acc[...] + d_ref[...] * u.astype(F32))
    lin = jnp.dot(y.astype(BF16), wglu_ref[...], preferred_element_type=F32)
    out = y * jax.nn.sigmoid(lin) * _silu(z_ref[...].reshape(rows, d).astype(F32))
    o_ref[...] = out.astype(o_ref.dtype).reshape(tt, bsz, d)


def _s5_tables(a_re, a_im, log_dt, b_re, b_im, c_re, c_im):
    g, n, p = S5_GROUPS, S5_STATE, S5_GROUP
    gt, nt = S5_GROUPS_PER_TILE, S5_TILES
    a = lax.complex(a_re.astype(F32), a_im.astype(F32))
    dt = jnp.exp(log_dt.astype(F32))[:, None]
    a_bar = jnp.exp(a * dt)
    b_bar = ((a_bar - 1.0) / a)[..., None] * lax.complex(b_re.astype(F32), b_im.astype(F32))
    eye = jnp.eye(gt, dtype=F32)

    def in_block(b):
        return jnp.einsum('jgnp,gh->jgphn', b.reshape(nt, gt, n, p), eye).reshape(nt, gt * p, gt * n)

    def out_block(c):
        return jnp.einsum('jgpn,gh->jgnhp', c.reshape(nt, gt, p, n), eye).reshape(nt, gt * n, gt * p)

    bmat = jnp.concatenate([in_block(b_bar.real), in_block(b_bar.imag)], axis=-1)
    cmat = jnp.concatenate([out_block(c_re.astype(F32)), out_block(-c_im.astype(F32))], axis=1)
    a_cols = lambda v: v.reshape(nt, 1, gt * n)
    return bmat.astype(BF16), cmat.astype(BF16), a_cols(a_bar.real), a_cols(a_bar.imag)


def _s5(u, z, tables, d_skip, w_glu):
    seq, bsz, d = u.shape
    tt = min(TIME_TILE, seq)
    rows = tt * bsz
    bmat, cmat, a_re, a_im = tables
    whole = lambda arr: pl.BlockSpec(arr.shape, lambda i: (0,) * arr.ndim)
    return pl.pallas_call(
        _s5_kernel,
        grid=(seq // tt,),
        in_specs=[
            pl.BlockSpec((tt, bsz, d), lambda i: (i, 0, 0)),
            pl.BlockSpec((tt, bsz, d), lambda i: (i, 0, 0)),
            whole(bmat), whole(cmat), whole(a_re), whole(a_im),
            pl.BlockSpec((1, d), lambda i: (0, 0)),
            pl.BlockSpec((d, d), lambda i: (0, 0)),
        ],
        out_specs=pl.BlockSpec((tt, bsz, d), lambda i: (i, 0, 0)),
        out_shape=jax.ShapeDtypeStruct((seq, bsz, d), BF16),
        scratch_shapes=[
            pltpu.VMEM((bsz, S5_GROUPS * S5_STATE), F32),
            pltpu.VMEM((bsz, S5_GROUPS * S5_STATE), F32),
            pltpu.VMEM((rows, S5_TILE_ST), F32),
            pltpu.VMEM((rows, S5_TILE_ST), F32),
            pltpu.VMEM((rows, d), F32),
        ],
        compiler_params=_params("arbitrary"),
        name="s5",
    )(u, z, bmat, cmat, a_re, a_im, d_skip.astype(F32).reshape(1, d), w_glu.astype(BF16))


def _mlstm_kernel(mi_ref, mz_ref, convw_ref, convb_ref, wqk_ref, wvo_ref, wif_ref, wift_ref,
                  bif_ref, bift_ref, ng_ref, skip_ref, o_ref,
                  ext, c_st, n_st, m_st):
    t = CHUNK
    dh = M_HEAD_DIM

    @pl.when(pl.program_id(1) == 0)
    def _():
        ext[0:CONV_TAIL, :] = jnp.zeros((CONV_TAIL, D_MODEL), F32)
        c_st[...] = jnp.zeros_like(c_st)
        n_st[...] = jnp.zeros_like(n_st)
        m_st[...] = jnp.zeros_like(m_st)

    mi_b = mi_ref[...]
    mi = mi_b.astype(F32)
    ext[CONV_TAIL:, :] = mi
    conv = convb_ref[...] + mi * convw_ref[M_CONV - 1:M_CONV, :]
    for k in range(M_CONV - 1):
        off = CONV_TAIL - (M_CONV - 1) + k
        conv = conv + ext[off:off + t, :] * convw_ref[k:k + 1, :]
    ext[0:CONV_TAIL, :] = mi[t - CONV_TAIL:, :]
    xc = _silu(conv)
    xc_b = xc.astype(BF16)

    g_col = (jnp.dot(xc_b, wif_ref[0:D_MODEL, :], preferred_element_type=F32)
             + jnp.dot(mi_b, wif_ref[D_MODEL:, :], preferred_element_type=F32) + bif_ref[...])
    nt_dims = (((1,), (1,)), ((), ()))
    g_row = (lax.dot_general(wift_ref[:, 0:D_MODEL], xc_b, nt_dims, preferred_element_type=F32)
             + lax.dot_general(wift_ref[:, D_MODEL:], mi_b, nt_dims, preferred_element_type=F32)
             + bift_ref[...])
    lf_col = jax.nn.log_sigmoid(g_col[:, M_HEADS:])
    lf_row = jax.nn.log_sigmoid(g_row[M_HEADS:, :])
    r_idx = lax.broadcasted_iota(jnp.int32, (t, t), 0)
    c_idx = lax.broadcasted_iota(jnp.int32, (t, t), 1)
    tri = r_idx >= c_idx
    b_col = jnp.dot(tri.astype(F32), lf_col, preferred_element_type=F32, precision=HIGHEST)
    b_row = jnp.dot(lf_row, (r_idx <= c_idx).astype(F32), preferred_element_type=F32, precision=HIGHEST)

    mz = mz_ref[...].astype(F32)
    for h in range(M_HEADS):
        hs = slice(h * dh, (h + 1) * dh)
        qk = jnp.dot(xc_b[:, hs], wqk_ref[h], preferred_element_type=F32)
        vo = jnp.dot(mi_b[:, hs], wvo_ref[h], preferred_element_type=F32)
        q, k = qk[:, :dh], qk[:, dh:] * (dh ** -0.5)
        v, o = vo[:, :dh], jax.nn.sigmoid(vo[:, dh:])
        q_b, k_b, v_b = q.astype(BF16), k.astype(BF16), v.astype(BF16)

        bc = b_col[:, h:h + 1]
        br = b_row[h:h + 1, :]
        li_r = g_row[h:h + 1, :]
        m_prev = m_st[h]
        d_log = jnp.where(tri, bc - br + li_r, -jnp.inf)
        inter = bc + m_prev
        m_t = jnp.maximum(inter, jnp.max(d_log, axis=-1, keepdims=True))
        s = lax.dot_general(q_b, k_b, nt_dims, preferred_element_type=F32) * jnp.exp(d_log - m_t)
        g = jnp.exp(inter - m_t)
        c_prev = c_st[h]
        n_prev = n_st[h]
        num = (jnp.dot(s.astype(BF16), v_b, preferred_element_type=F32)
               + g * jnp.dot(q_b, c_prev.astype(BF16), preferred_element_type=F32))
        den = jnp.sum(s, axis=-1, keepdims=True) + g * jnp.sum(q * n_prev, axis=-1, keepdims=True)
        hh = num / jnp.maximum(jnp.abs(den), jnp.exp(-m_t))

        b_last = br[:, t - 1:t]
        w_col = b_last - bc + g_col[:, h:h + 1]
        m_new = jnp.maximum(b_last + m_prev, jnp.max(w_col, axis=0, keepdims=True))
        kw = k * jnp.exp(w_col - m_new)
        decay = jnp.exp(b_last + m_prev - m_new)
        c_st[h] = decay * c_prev + lax.dot_general(
            kw.astype(BF16), v_b, (((0,), (0,)), ((), ())), preferred_element_type=F32)
        n_st[h] = decay * n_prev + jnp.sum(kw, axis=0, keepdims=True)
        m_st[h] = m_new

        oh = o * hh
        ms = jnp.mean(oh * oh, axis=-1, keepdims=True)
        y = oh * lax.rsqrt(ms + EPS) * ng_ref[:, hs] + skip_ref[:, hs] * xc[:, hs]
        o_ref[:, hs] = (y * _silu(mz[:, hs])).astype(o_ref.dtype)


def _mlstm(m_in, m_z, conv_w, conv_b, wq, wk, wv, wo, w_if, b_if, norm_g, skip):
    seq, bsz, d = m_in.shape
    flat = lambda a: a.reshape(seq, bsz * a.shape[-1])
    wqk = jnp.concatenate([wq, wk], axis=-1).astype(BF16)
    wvo = jnp.concatenate([wv, wo], axis=-1).astype(BF16)
    w_if = w_if.astype(BF16)
    b_if = b_if.astype(F32)
    whole = lambda arr: pl.BlockSpec(arr.shape, lambda b, c: (0,) * arr.ndim)
    args = [conv_w.astype(F32), conv_b.astype(F32).reshape(1, d), wqk, wvo, w_if, w_if.T,
            b_if.reshape(1, -1), b_if.reshape(-1, 1), norm_g.astype(F32).reshape(1, d),
            skip.astype(F32).reshape(1, d)]
    return pl.pallas_call(
        _mlstm_kernel,
        grid=(bsz, seq // CHUNK),
        in_specs=[
            pl.BlockSpec((CHUNK, d), lambda b, c: (c, b)),
            pl.BlockSpec((CHUNK, d), lambda b, c: (c, b)),
        ] + [whole(a) for a in args],
        out_specs=pl.BlockSpec((CHUNK, d), lambda b, c: (c, b)),
        out_shape=jax.ShapeDtypeStruct((seq, bsz * d), BF16),
        scratch_shapes=[
            pltpu.VMEM((CONV_TAIL + CHUNK, d), F32),
            pltpu.VMEM((M_HEADS, M_HEAD_DIM, M_HEAD_DIM), F32),
            pltpu.VMEM((M_HEADS, 1, M_HEAD_DIM), F32),
            pltpu.VMEM((M_HEADS, 1, 1), F32),
        ],
        compiler_params=_params("parallel", "arbitrary"),
        name="mlstm",
    )(flat(m_in), flat(m_z), *args).reshape(seq, bsz, d)


def _rope_tables(pos):
    lane = lax.broadcasted_iota(jnp.int32, (1, LANES), 1)
    in_head = lane % A_HEAD_DIM
    freq = (lane % ROPE_HALF).astype(F32)
    inv_freq = jnp.exp(-math.log(ROPE_THETA) * freq / ROPE_HALF)
    ang = pos * inv_freq
    cos, sin = jnp.cos(ang), jnp.sin(ang)
    first, second = in_head < ROPE_HALF, (in_head >= ROPE_HALF) & (in_head < ROPE_DIM)
    cos_t = jnp.where(first | second, cos, 1.0)
    sin_up = jnp.where(first, -sin, 0.0)
    sin_dn = jnp.where(second, sin, 0.0)
    return cos_t, sin_up, sin_dn


def _norm_rope(x, gain, seg_mean, tables):
    cos_t, sin_up, sin_dn = tables
    ms = jnp.dot((x * x).astype(BF16), seg_mean, preferred_element_type=F32)
    xn = x * lax.rsqrt(ms + EPS) * gain
    return (xn * cos_t + pltpu.roll(xn, LANES - ROPE_HALF, 1) * sin_up
            + pltpu.roll(xn, ROPE_HALF, 1) * sin_dn)


def _swa_kernel(q_ref, z_ref, kc_ref, kp_ref, vc_ref, vp_ref, posc_ref, posp_ref,
                qg_ref, kg_ref, sink_ref, o_ref):
    t = CHUNK
    hd = A_HEAD_DIM
    blk = pl.program_id(1)
    lane_r = lax.broadcasted_iota(jnp.int32, (LANES, LANES), 0)
    lane_c = lax.broadcasted_iota(jnp.int32, (LANES, LANES), 1)
    seg_mean = jnp.where(lane_r // hd == lane_c // hd, 1.0 / hd, 0.0).astype(BF16)
    own_col = lax.broadcasted_iota(jnp.int32, posc_ref.shape, 1) == pl.program_id(0)
    pick = lambda p_ref: jnp.sum(jnp.where(own_col, p_ref[...], 0.0), axis=-1, keepdims=True)
    tab_c = _rope_tables(pick(posc_ref))
    tab_p = _rope_tables(pick(posp_ref))
    lane = lax.broadcasted_iota(jnp.int32, (1, LANES), 1)
    low_half = lane < hd

    qi = lax.broadcasted_iota(jnp.int32, (t, 2 * t), 0) + t
    ki = lax.broadcasted_iota(jnp.int32, (t, 2 * t), 1)
    rel = qi - ki
    valid = (rel >= 0) & (rel < t) & ((ki >= t) | (blk > 0))
    scale = hd ** -0.5
    nt_dims = (((1,), (1,)), ((), ()))

    for kv in range(A_KV_HEADS):
        ks = slice((kv // 2) * LANES, (kv // 2 + 1) * LANES)
        k_cat = jnp.concatenate([
            _norm_rope(kp_ref[:, ks].astype(F32), kg_ref[...], seg_mean, tab_p),
            _norm_rope(kc_ref[:, ks].astype(F32), kg_ref[...], seg_mean, tab_c)], axis=0)
        v_cat = jnp.concatenate([vp_ref[:, ks], vc_ref[:, ks]], axis=0).astype(F32)
        own_low = (kv % 2 == 0)
        keep = low_half if own_low else jnp.logical_not(low_half)
        k_own = jnp.where(keep, k_cat, 0.0)
        v_own = jnp.where(keep, v_cat, 0.0)
        k_oth = pltpu.roll(k_own, hd, 1)
        v_oth = pltpu.roll(v_own, hd, 1)
        k_lo, k_hi = (k_own, k_oth) if own_low else (k_oth, k_own)
        v_lo, v_hi = (v_own, v_oth) if own_low else (v_oth, v_own)
        k_lo, k_hi = k_lo.astype(BF16), k_hi.astype(BF16)
        v_lo, v_hi = v_lo.astype(BF16), v_hi.astype(BF16)
        for pair in range(2):
            slab = 2 * kv + pair
            qs = slice(slab * LANES, (slab + 1) * LANES)
            qn = _norm_rope(q_ref[:, qs].astype(F32), qg_ref[...], seg_mean, tab_c).astype(BF16)
            out = None
            for half, (k_h, v_h) in enumerate(((k_lo, v_lo), (k_hi, v_hi))):
                head = 2 * slab + half
                sc = lax.dot_general(qn, k_h, nt_dims, preferred_element_type=F32) * scale
                sc = jnp.where(valid, sc, NEG_BIG)
                sink = sink_ref[:, head:head + 1]
                m = jnp.maximum(jnp.max(sc, axis=-1, keepdims=True), sink)
                p = jnp.exp(sc - m)
                denom = jnp.sum(p, axis=-1, keepdims=True) + jnp.exp(sink - m)
                pv = jnp.dot((p / denom).astype(BF16), v_h, preferred_element_type=F32)
                out = pv if out is None else out + pv
            o_ref[:, qs] = (out * _silu(z_ref[:, qs].astype(F32))).astype(o_ref.dtype)


def _swa(q, z, kv, pos, q_g, k_g, sinks):
    seq, bsz, d = q.shape
    flat = lambda a: a.reshape(seq, bsz * a.shape[-1])
    prev = lambda c: jnp.maximum(c - 1, 0)
    tile2 = lambda g: jnp.tile(g.astype(F32).reshape(1, A_HEAD_DIM), (1, LANES // A_HEAD_DIM))
    whole = lambda arr: pl.BlockSpec(arr.shape, lambda b, c: (0,) * arr.ndim)
    args = [tile2(q_g), tile2(k_g), sinks.astype(F32).reshape(1, A_HEADS)]
    return pl.pallas_call(
        _swa_kernel,
        grid=(bsz, seq // CHUNK),
        in_specs=[
            pl.BlockSpec((CHUNK, d), lambda b, c: (c, b)),
            pl.BlockSpec((CHUNK, d), lambda b, c: (c, b)),
            pl.BlockSpec((CHUNK, A_KV), lambda b, c: (c, 2 * b)),
            pl.BlockSpec((CHUNK, A_KV), lambda b, c: (prev(c), 2 * b)),
            pl.BlockSpec((CHUNK, A_KV), lambda b, c: (c, 2 * b + 1)),
            pl.BlockSpec((CHUNK, A_KV), lambda b, c: (prev(c), 2 * b + 1)),
            pl.BlockSpec((CHUNK, bsz), lambda b, c: (c, 0)),
            pl.BlockSpec((CHUNK, bsz), lambda b, c: (prev(c), 0)),
        ] + [whole(a) for a in args],
        out_specs=pl.BlockSpec((CHUNK, d), lambda b, c: (c, b)),
        out_shape=jax.ShapeDtypeStruct((seq, bsz * d), BF16),
        compiler_params=_params("parallel", "parallel"),
        name="swa",
    )(flat(q), flat(z), flat(kv), flat(kv), flat(kv), flat(kv), pos, pos, *args).reshape(seq, bsz, d)


def kernel(x, c, positions, ada_w, ada_b, norm_g, ev_w_in, s5_a_re, s5_a_im, s5_log_dt, s5_b_re, s5_b_im, s5_c_re, s5_c_im, s5_d, s5_w_glu, m_conv_w, m_conv_b, m_wq, m_wk, m_wv, m_wo, m_w_if, m_b_if, m_norm_g, m_skip, ev_w_out, od_w_in, od_q_norm_g, od_k_norm_g, od_sinks, od_w_out):
    bsz, seq, d = x.shape
    assert d == D_MODEL and seq % CHUNK == 0 and bsz % 16 == 0
    mod = _adaln(c, ada_w, ada_b)
    h = jnp.transpose(x.astype(F32), (1, 0, 2))
    pos = jnp.transpose(positions.astype(F32), (1, 0))
    a_q = A_HEADS * A_HEAD_DIM
    for layer in range(DEPTH):
        i = layer // 2
        if layer % 2 == 0:
            s5_u, s5_z, m_in, m_z = _inproj(h, mod, layer, norm_g[layer], ev_w_in[i], [d] * 4)
            tables = _s5_tables(s5_a_re[i], s5_a_im[i], s5_log_dt[i], s5_b_re[i], s5_b_im[i],
                                s5_c_re[i], s5_c_im[i])
            y_s5 = _s5(s5_u, s5_z, tables, s5_d[i], s5_w_glu[i])
            y_m = _mlstm(m_in, m_z, m_conv_w[i], m_conv_b[i], m_wq[i], m_wk[i], m_wv[i], m_wo[i],
                         m_w_if[i], m_b_if[i], m_norm_g[i], m_skip[i])
            h = _outproj(h, mod, layer, [y_s5, y_m], ev_w_out[i])
        else:
            w = od_w_in[i]
            w = jnp.concatenate([w[:, :a_q], w[:, a_q + 2 * A_KV:], w[:, a_q:a_q + 2 * A_KV]], axis=1)
            q, z, kv = _inproj(h, mod, layer, norm_g[layer], w, [a_q, a_q, 2 * A_KV])
            attn = _swa(q, z, kv, pos, od_q_norm_g[i], od_k_norm_g[i], od_sinks[i])
            h = _outproj(h, mod, layer, [attn], od_w_out[i])
    return jnp.transpose(h, (1, 0, 2)).astype(x.dtype)
```

```python
import functools
import math

import jax
import jax.numpy as jnp
from jax import lax
from jax.experimental import pallas as pl
from jax.experimental.pallas import tpu as pltpu

F32 = jnp.float32
BF16 = jnp.bfloat16
HIGHEST = lax.Precision.HIGHEST

EPS = 1e-6
D_MODEL = 1024
DEPTH = 4
S5_GROUP = 16
S5_STATE = 64
S5_GROUPS = D_MODEL // S5_GROUP
S5_GROUPS_PER_TILE = 16
S5_TILES = S5_GROUPS // S5_GROUPS_PER_TILE
S5_TILE_CH = S5_GROUPS_PER_TILE * S5_GROUP
S5_TILE_ST = S5_GROUPS_PER_TILE * S5_STATE
S5_SCAN_COLS = 256
S5_TIME_TILE = 32
S5_PERM_T = 16
M_HEADS = 4
M_HEAD_DIM = D_MODEL // M_HEADS
M_CONV = 4
CHUNK = 128
CONV_TAIL = 8
A_HEAD_DIM = 64
A_HEADS = D_MODEL // A_HEAD_DIM
A_KV_HEADS = A_HEADS // 4
A_KV = A_KV_HEADS * A_HEAD_DIM
ROPE_THETA = 500000.0
ROPE_DIM = A_HEAD_DIM // 4
ROPE_HALF = ROPE_DIM // 2
LANES = 128
NEG_BIG = -1e30

ROW_TILE = 512
VMEM_LIMIT = 56 * 1024 * 1024
NT_DIMS = (((1,), (1,)), ((), ()))


def _silu(x):
    return x * jax.nn.sigmoid(x)


def _gelu_tanh(x):
    return 0.5 * x * (1.0 + jnp.tanh(math.sqrt(2.0 / math.pi) * (x + 0.044715 * (x * x * x))))


def _params(*sem):
    return pltpu.CompilerParams(dimension_semantics=sem, vmem_limit_bytes=VMEM_LIMIT)


def _resident(arr, n_grid):
    zeros = (0,) * arr.ndim
    index_map = (lambda i: zeros) if n_grid == 1 else (lambda i, j: zeros)
    return pl.BlockSpec(arr.shape, index_map, pipeline_mode=pl.Buffered(1))


def _adaln_kernel(c_ref, w_ref, b_ref, o_ref):
    s = _silu(c_ref[...])
    o_ref[...] = jnp.dot(s, w_ref[...], preferred_element_type=F32, precision=HIGHEST) + b_ref[...]


def _adaln(c, ada_w, ada_b):
    bsz = c.shape[0]
    mod = pl.pallas_call(
        _adaln_kernel,
        grid=(DEPTH, 3),
        in_specs=[
            pl.BlockSpec((bsz, D_MODEL), lambda l, j: (0, 0)),
            pl.BlockSpec((None, D_MODEL, D_MODEL), lambda l, j: (l, 0, j)),
            pl.BlockSpec((None, None, 1, D_MODEL), lambda l, j: (l, j, 0, 0)),
        ],
        out_specs=pl.BlockSpec((None, None, bsz, D_MODEL), lambda l, j: (l, j, 0, 0)),
        out_shape=jax.ShapeDtypeStruct((DEPTH, 3, bsz, D_MODEL), F32),
        compiler_params=_params("parallel", "parallel"),
        name="adaln",
    )(c.astype(F32), ada_w.astype(F32), ada_b.astype(F32).reshape(DEPTH, 3, 1, D_MODEL))
    return jnp.transpose(mod, (0, 2, 1, 3))


def _mod_spec(layer):
    return pl.BlockSpec((None, None, 3, D_MODEL), lambda b, i: (layer, b, 0, 0))


def _inproj_kernel(h_ref, mod_ref, g_ref, w_ref, *o_refs):
    x = h_ref[...]
    ms = jnp.mean(x * x, axis=-1, keepdims=True)
    hn = x * lax.rsqrt(ms + EPS) * g_ref[...] * (1.0 + mod_ref[1:2, :]) + mod_ref[0:1, :]
    hb = hn.astype(BF16)
    c0 = 0
    for o_ref in o_refs:
        width = o_ref.shape[-1]
        o_ref[...] = jnp.dot(hb, w_ref[:, c0:c0 + width], preferred_element_type=F32).astype(o_ref.dtype)
        c0 += width


def _inproj(h, mod, layer, g, w, widths):
    bsz, seq, d = h.shape
    assert sum(widths) == w.shape[1]
    rt = min(ROW_TILE, seq)
    w = w.astype(BF16)
    return pl.pallas_call(
        _inproj_kernel,
        grid=(bsz, seq // rt),
        in_specs=[
            pl.BlockSpec((None, rt, d), lambda b, i: (b, i, 0)),
            _mod_spec(layer),
            pl.BlockSpec((1, d), lambda b, i: (0, 0)),
            _resident(w, 2),
        ],
        out_specs=[pl.BlockSpec((None, rt, wd), lambda b, i: (b, i, 0)) for wd in widths],
        out_shape=[jax.ShapeDtypeStruct((bsz, seq, wd), BF16) for wd in widths],
        compiler_params=_params("parallel", "parallel"),
        name="inproj",
    )(h, mod, g.astype(F32).reshape(1, d), w)


def _outproj_kernel(*refs, n_parts):
    h_ref, mod_ref = refs[0], refs[1]
    y_refs = refs[2:2 + n_parts]
    w_ref, o_ref = refs[2 + n_parts], refs[3 + n_parts]
    acc = None
    for i, y_ref in enumerate(y_refs):
        k = y_ref.shape[-1]
        part = jnp.dot(y_ref[...], w_ref[i * k:(i + 1) * k, :], preferred_element_type=F32)
        acc = part if acc is None else acc + part
    o_ref[...] = h_ref[...] + mod_ref[2:3, :] * acc


def _outproj(h, mod, layer, parts, w):
    bsz, seq, d = h.shape
    rt = min(ROW_TILE, seq)
    act_spec = lambda width: pl.BlockSpec((None, rt, width), lambda b, i: (b, i, 0))
    w = w.astype(BF16)
    return pl.pallas_call(
        functools.partial(_outproj_kernel, n_parts=len(parts)),
        grid=(bsz, seq // rt),
        in_specs=[act_spec(d), _mod_spec(layer)] + [act_spec(p.shape[-1]) for p in parts] + [_resident(w, 2)],
        out_specs=act_spec(d),
        out_shape=jax.ShapeDtypeStruct((bsz, seq, d), F32),
        compiler_params=_params("parallel", "parallel"),
        name="outproj",
    )(h, mod, *parts, w)


def _s5_kernel(u_ref, z_ref, perm_ref, permt_ref, bmat_ref, cmat_ref, are_ref, aim_ref, d_ref, wglu_ref,
               o_ref, st_re, st_im, u_tm, z_tm, x_re, x_im, y_acc):
    bsz, tt, d = u_ref.shape
    rows = tt * bsz
    pt = S5_PERM_T
    blk_rows = pt * bsz

    @pl.when(pl.program_id(0) == 0)
    def _():
        st_re[...] = jnp.zeros_like(st_re)
        st_im[...] = jnp.zeros_like(st_im)

    for src_ref, dst in ((u_ref, u_tm), (z_ref, z_tm)):
        for tb in range(tt // pt):
            by_seq = jnp.concatenate([src_ref[b, tb * pt:(tb + 1) * pt, :] for b in range(bsz)], axis=0)
            dst[tb * blk_rows:(tb + 1) * blk_rows, :] = jnp.dot(
                perm_ref[...], by_seq, preferred_element_type=F32).astype(BF16)

    u = u_tm[...]
    for j in range(S5_TILES):
        u_j = u[:, j * S5_TILE_CH:(j + 1) * S5_TILE_CH]
        x_re[...] = jnp.dot(u_j, bmat_ref[j, :, :S5_TILE_ST], preferred_element_type=F32)
        x_im[...] = jnp.dot(u_j, bmat_ref[j, :, S5_TILE_ST:], preferred_element_type=F32)
        for c0 in range(0, S5_TILE_ST, S5_SCAN_COLS):
            cols = slice(c0, c0 + S5_SCAN_COLS)
            st_cols = slice(j * S5_TILE_ST + c0, j * S5_TILE_ST + c0 + S5_SCAN_COLS)
            a_re = jnp.broadcast_to(are_ref[j, :, cols], (bsz, S5_SCAN_COLS))
            a_im = jnp.broadcast_to(aim_ref[j, :, cols], (bsz, S5_SCAN_COLS))

            def step(t, carry, cols=cols, a_re=a_re, a_im=a_im):
                s_re, s_im = carry
                r0 = pl.multiple_of(t * bsz, bsz)
                n_re = a_re * s_re - a_im * s_im + x_re[pl.ds(r0, bsz), cols]
                n_im = a_re * s_im + a_im * s_re + x_im[pl.ds(r0, bsz), cols]
                x_re[pl.ds(r0, bsz), cols] = n_re
                x_im[pl.ds(r0, bsz), cols] = n_im
                return n_re, n_im

            s_re, s_im = lax.fori_loop(0, tt, step, (st_re[:, st_cols], st_im[:, st_cols]))
            st_re[:, st_cols] = s_re
            st_im[:, st_cols] = s_im
        y_acc[:, j * S5_TILE_CH:(j + 1) * S5_TILE_CH] = (
            jnp.dot(x_re[...].astype(BF16), cmat_ref[j, :S5_TILE_ST, :], preferred_element_type=F32)
            + jnp.dot(x_im[...].astype(BF16), cmat_ref[j, S5_TILE_ST:, :], preferred_element_type=F32))

    y = _gelu_tanh(y_acc[...] + d_ref[...] * u.astype(F32))
    lin = jnp.dot(y.astype(BF16), wglu_ref[...], preferred_element_type=F32)
    out = (y * jax.nn.sigmoid(lin) * _silu(z_tm[...].astype(F32))).astype(BF16)
    for tb in range(tt // pt):
        by_seq = jnp.dot(permt_ref[...], out[tb * blk_rows:(tb + 1) * blk_rows, :],
                         preferred_element_type=F32).astype(o_ref.dtype)
        for b in range(bsz):
            o_ref[b, tb * pt:(tb + 1) * pt, :] = by_seq[b * pt:(b + 1) * pt, :]


def _s5_tables(a_re, a_im, log_dt, b_re, b_im, c_re, c_im):
    g, n, p = S5_GROUPS, S5_STATE, S5_GROUP
    gt, nt = S5_GROUPS_PER_TILE, S5_TILES
    a = lax.complex(a_re.astype(F32), a_im.astype(F32))
    dt = jnp.exp(log_dt.astype(F32))[:, None]
    a_bar = jnp.exp(a * dt)
    b_bar = ((a_bar - 1.0) / a)[..., None] * lax.complex(b_re.astype(F32), b_im.astype(F32))
    eye = jnp.eye(gt, dtype=F32)

    def in_block(b):
        return jnp.einsum('jgnp,gh->jgphn', b.reshape(nt, gt, n, p), eye).reshape(nt, gt * p, gt * n)

    def out_block(c):
        return jnp.einsum('jgpn,gh->jgnhp', c.reshape(nt, gt, p, n), eye).reshape(nt, gt * n, gt * p)

    bmat = jnp.concatenate([in_block(b_bar.real), in_block(b_bar.imag)], axis=-1)
    cmat = jnp.concatenate([out_block(c_re.astype(F32)), out_block(-c_im.astype(F32))], axis=1)
    a_cols = lambda v: v.reshape(nt, 1, gt * n)
    return bmat.astype(BF16), cmat.astype(BF16), a_cols(a_bar.real), a_cols(a_bar.imag)


def _s5(u, z, tables, d_skip, w_glu):
    bsz, seq, d = u.shape
    tt = min(S5_TIME_TILE, seq)
    rows = tt * bsz
    bmat, cmat, a_re, a_im = tables
    blk_rows = S5_PERM_T * bsz
    perm = jnp.eye(blk_rows, dtype=BF16).reshape(bsz, S5_PERM_T, blk_rows).transpose(1, 0, 2).reshape(blk_rows, blk_rows)
    consts = [perm, perm.T, bmat, cmat, a_re, a_im, d_skip.astype(F32).reshape(1, d), w_glu.astype(BF16)]
    act_spec = pl.BlockSpec((bsz, tt, d), lambda i: (0, i, 0))
    return pl.pallas_call(
        _s5_kernel,
        grid=(seq // tt,),
        in_specs=[act_spec, act_spec] + [_resident(a, 1) for a in consts],
        out_specs=act_spec,
        out_shape=jax.ShapeDtypeStruct((bsz, seq, d), BF16),
        scratch_shapes=[
            pltpu.VMEM((bsz, S5_GROUPS * S5_STATE), F32),
            pltpu.VMEM((bsz, S5_GROUPS * S5_STATE), F32),
            pltpu.VMEM((rows, d), BF16),
            pltpu.VMEM((rows, d), BF16),
            pltpu.VMEM((rows, S5_TILE_ST), F32),
            pltpu.VMEM((rows, S5_TILE_ST), F32),
            pltpu.VMEM((rows, d), F32),
        ],
        compiler_params=_params("arbitrary"),
        name="s5",
    )(u, z, *consts)


def _mlstm_kernel(mi_ref, mz_ref, convw_ref, convb_ref, wqk_ref, wvo_ref, wif_ref, wift_ref,
                  bif_ref, bift_ref, ng_ref, skip_ref, o_ref,
                  ext, c_st, n_st, m_st):
    t = CHUNK
    dh = M_HEAD_DIM

    @pl.when(pl.program_id(1) == 0)
    def _():
        ext[0:CONV_TAIL, :] = jnp.zeros((CONV_TAIL, D_MODEL), F32)
        c_st[...] = jnp.zeros_like(c_st)
        n_st[...] = jnp.zeros_like(n_st)
        m_st[...] = jnp.zeros_like(m_st)

    mi_b = mi_ref[...]
    mi = mi_b.astype(F32)
    ext[CONV_TAIL:, :] = mi
    conv = convb_ref[...] + mi * convw_ref[M_CONV - 1:M_CONV, :]
    for k in range(M_CONV - 1):
        off = CONV_TAIL - (M_CONV - 1) + k
        conv = conv + ext[off:off + t, :] * convw_ref[k:k + 1, :]
    ext[0:CONV_TAIL, :] = mi[t - CONV_TAIL:, :]
    xc = _silu(conv)
    xc_b = xc.astype(BF16)

    g_col = (jnp.dot(xc_b, wif_ref[0:D_MODEL, :], preferred_element_type=F32)
             + jnp.dot(mi_b, wif_ref[D_MODEL:, :], preferred_element_type=F32) + bif_ref[...])
    g_row = (lax.dot_general(wift_ref[:, 0:D_MODEL], xc_b, NT_DIMS, preferred_element_type=F32)
             + lax.dot_general(wift_ref[:, D_MODEL:], mi_b, NT_DIMS, preferred_element_type=F32)
             + bift_ref[...])
    lf_col = jax.nn.log_sigmoid(g_col[:, M_HEADS:])
    lf_row = jax.nn.log_sigmoid(g_row[M_HEADS:, :])
    r_idx = lax.broadcasted_iota(jnp.int32, (t, t), 0)
    c_idx = lax.broadcasted_iota(jnp.int32, (t, t), 1)
    tri = r_idx >= c_idx
    b_col = jnp.dot(tri.astype(F32), lf_col, preferred_element_type=F32, precision=HIGHEST)
    b_row = jnp.dot(lf_row, (r_idx <= c_idx).astype(F32), preferred_element_type=F32, precision=HIGHEST)

    heads = range(M_HEADS)
    hsl = [slice(h * dh, (h + 1) * dh) for h in heads]
    qk = [jnp.dot(xc_b[:, hsl[h]], wqk_ref[h], preferred_element_type=F32) for h in heads]
    vo = [jnp.dot(mi_b[:, hsl[h]], wvo_ref[h], preferred_element_type=F32) for h in heads]
    q = [qk[h][:, :dh] for h in heads]
    k = [qk[h][:, dh:] * (dh ** -0.5) for h in heads]
    q_b = [q[h].astype(BF16) for h in heads]
    v_b = [vo[h][:, :dh].astype(BF16) for h in heads]
    c_prev = [c_st[h] for h in heads]
    n_prev = [n_st[h] for h in heads]
    m_prev = [m_st[h] for h in heads]
    qkt = [lax.dot_general(q_b[h], k[h].astype(BF16), NT_DIMS, preferred_element_type=F32) for h in heads]
    qc = [jnp.dot(q_b[h], c_prev[h].astype(BF16), preferred_element_type=F32) for h in heads]

    s, g, m_t, kw, decay = [], [], [], [], []
    for h in heads:
        bc = b_col[:, h:h + 1]
        br = b_row[h:h + 1, :]
        d_log = jnp.where(tri, bc - br + g_row[h:h + 1, :], -jnp.inf)
        inter = bc + m_prev[h]
        m_t.append(jnp.maximum(inter, jnp.max(d_log, axis=-1, keepdims=True)))
        s.append(qkt[h] * jnp.exp(d_log - m_t[h]))
        g.append(jnp.exp(inter - m_t[h]))
        b_last = br[:, t - 1:t]
        w_col = b_last - bc + g_col[:, h:h + 1]
        m_new = jnp.maximum(b_last + m_prev[h], jnp.max(w_col, axis=0, keepdims=True))
        kw.append(k[h] * jnp.exp(w_col - m_new))
        decay.append(jnp.exp(b_last + m_prev[h] - m_new))
        m_st[h] = m_new

    sv = [jnp.dot(s[h].astype(BF16), v_b[h], preferred_element_type=F32) for h in heads]
    kv = [lax.dot_general(kw[h].astype(BF16), v_b[h], (((0,), (0,)), ((), ())), preferred_element_type=F32)
          for h in heads]

    mz = mz_ref[...].astype(F32)
    for h in heads:
        c_st[h] = decay[h] * c_prev[h] + kv[h]
        n_st[h] = decay[h] * n_prev[h] + jnp.sum(kw[h], axis=0, keepdims=True)
        num = sv[h] + g[h] * qc[h]
        den = jnp.sum(s[h], axis=-1, keepdims=True) + g[h] * jnp.sum(q[h] * n_prev[h], axis=-1, keepdims=True)
        oh = jax.nn.sigmoid(vo[h][:, dh:]) * (num / jnp.maximum(jnp.abs(den), jnp.exp(-m_t[h])))
        ms = jnp.mean(oh * oh, axis=-1, keepdims=True)
        y = oh * lax.rsqrt(ms + EPS) * ng_ref[:, hsl[h]] + skip_ref[:, hsl[h]] * xc[:, hsl[h]]
        o_ref[:, hsl[h]] = (y * _silu(mz[:, hsl[h]])).astype(o_ref.dtype)


def _mlstm(m_in, m_z, conv_w, conv_b, wq, wk, wv, wo, w_if, b_if, norm_g, skip):
    bsz, seq, d = m_in.shape
    wqk = jnp.concatenate([wq, wk], axis=-1).astype(BF16)
    wvo = jnp.concatenate([wv, wo], axis=-1).astype(BF16)
    w_if = w_if.astype(BF16)
    b_if = b_if.astype(F32)
    consts = [conv_w.astype(F32), conv_b.astype(F32).reshape(1, d), wqk, wvo, w_if, w_if.T,
              b_if.reshape(1, -1), b_if.reshape(-1, 1), norm_g.astype(F32).reshape(1, d),
              skip.astype(F32).reshape(1, d)]
    act_spec = pl.BlockSpec((None, CHUNK, d), lambda b, c: (b, c, 0))
    return pl.pallas_call(
        _mlstm_kernel,
        grid=(bsz, seq // CHUNK),
        in_specs=[act_spec, act_spec] + [_resident(a, 2) for a in consts],
        out_specs=act_spec,
        out_shape=jax.ShapeDtypeStruct((bsz, seq, d), BF16),
        scratch_shapes=[
            pltpu.VMEM((CONV_TAIL + CHUNK, d), F32),
            pltpu.VMEM((M_HEADS, M_HEAD_DIM, M_HEAD_DIM), F32),
            pltpu.VMEM((M_HEADS, 1, M_HEAD_DIM), F32),
            pltpu.VMEM((M_HEADS, 1, 1), F32),
        ],
        compiler_params=_params("parallel", "arbitrary"),
        name="mlstm",
    )(m_in, m_z, *consts)


def _rope_tables(pos):
    lane = lax.broadcasted_iota(jnp.int32, (1, LANES), 1)
    in_head = lane % A_HEAD_DIM
    freq = (lane % ROPE_HALF).astype(F32)
    inv_freq = jnp.exp(-math.log(ROPE_THETA) * freq / ROPE_HALF)
    ang = pos * inv_freq
    cos, sin = jnp.cos(ang), jnp.sin(ang)
    first, second = in_head < ROPE_HALF, (in_head >= ROPE_HALF) & (in_head < ROPE_DIM)
    cos_t = jnp.where(first | second, cos, 1.0)
    sin_t = jnp.where(first, -sin, jnp.where(second, sin, 0.0))
    return cos_t, sin_t


def _norm_rope(x, gain, mean_rot, tables):
    cos_t, sin_t = tables
    t = x.shape[0]
    n = x.shape[1] // LANES
    xs = jnp.concatenate([x[:, s * LANES:(s + 1) * LANES] for s in range(n)], axis=0)
    xg = xs * gain
    both = jnp.dot(jnp.concatenate([(xs * xs).astype(BF16), xg.astype(BF16)], axis=1), mean_rot,
                   preferred_element_type=F32)
    ms, partner = both[:, :LANES], both[:, LANES:]
    roped = xg.reshape(n, t, LANES) * cos_t + partner.reshape(n, t, LANES) * sin_t
    return roped.reshape(n * t, LANES) * lax.rsqrt(ms + EPS)


def _swa_kernel(q_ref, z_ref, kv_ref, pos_ref, bias_ref, mrot_ref, qg_ref, kg_ref, sink_ref, o_ref,
                k_buf, v_buf):
    t = CHUNK
    hd = A_HEAD_DIM
    blk = pl.program_id(1)
    low = lax.broadcasted_iota(jnp.int32, (1, LANES), 1) < hd

    @pl.when(blk == 0)
    def _():
        ones = jnp.broadcast_to(jnp.where(low, 1.0, 0.0), (2 * t, LANES))
        for kv in range(A_KV_HEADS):
            for part in range(2):
                k_buf[kv, 2 * part * t:(2 * part + 1) * t, :] = jnp.zeros((t, LANES), BF16)
                v_buf[kv, 2 * part * t:(2 * part + 1) * t, 0:LANES] = jnp.zeros((t, LANES), BF16)
                v_buf[kv, 2 * part * t:(2 * part + 2) * t, LANES:] = (ones if part == 0 else 1.0 - ones).astype(BF16)

    @pl.when(blk > 0)
    def _():
        for kv in range(A_KV_HEADS):
            for part in range(2):
                k_buf[kv, 2 * part * t:(2 * part + 1) * t, :] = k_buf[kv, (2 * part + 1) * t:(2 * part + 2) * t, :]
                v_buf[kv, 2 * part * t:(2 * part + 1) * t, 0:LANES] = (
                    v_buf[kv, (2 * part + 1) * t:(2 * part + 2) * t, 0:LANES])

    tables = _rope_tables(pos_ref[...])
    mean_rot = mrot_ref[...]

    kr = _norm_rope(kv_ref[:, 0:A_KV].astype(F32), kg_ref[...], mean_rot, tables)
    for s in range(A_KV // LANES):
        for src, buf in ((kr[s * t:(s + 1) * t], k_buf),
                         (kv_ref[:, A_KV + s * LANES:A_KV + (s + 1) * LANES].astype(F32), v_buf)):
            lo = jnp.where(low, src, 0.0)
            hi = jnp.where(low, 0.0, src)
            buf[2 * s, t:2 * t, 0:LANES] = lo.astype(BF16)
            buf[2 * s, 3 * t:4 * t, 0:LANES] = pltpu.roll(lo, hd, 1).astype(BF16)
            buf[2 * s + 1, t:2 * t, 0:LANES] = pltpu.roll(hi, hd, 1).astype(BF16)
            buf[2 * s + 1, 3 * t:4 * t, 0:LANES] = hi.astype(BF16)

    qr = _norm_rope(q_ref[...].astype(F32), qg_ref[...], mean_rot, tables).astype(BF16)
    bias = bias_ref[...]
    upper = lax.broadcasted_iota(jnp.int32, (2 * t, 1), 0) < t
    scores = [lax.dot_general(qr[2 * kv * t:(2 * kv + 2) * t], k_buf[kv], NT_DIMS, preferred_element_type=F32)
              for kv in range(A_KV_HEADS)]
    outs = []
    for kv in range(A_KV_HEADS):
        sc = (scores[kv].reshape(2, t, 4 * t) + bias).reshape(2 * t, 4 * t)
        probs, sink_terms = [], []
        for half in range(2):
            sc_h = sc[:, 2 * half * t:2 * (half + 1) * t]
            sink = jnp.where(upper, sink_ref[:, 4 * kv + half:4 * kv + half + 1],
                             sink_ref[:, 4 * kv + 2 + half:4 * kv + 3 + half])
            m = jnp.maximum(jnp.max(sc_h, axis=-1, keepdims=True), sink)
            probs.append(jnp.exp(sc_h - m).astype(BF16))
            sink_terms.append(jnp.exp(sink - m))
        o = jnp.dot(jnp.concatenate(probs, axis=1), v_buf[kv], preferred_element_type=F32)
        outs.append(o[:, :LANES] / (o[:, LANES:] + jnp.where(low, sink_terms[0], sink_terms[1])))
    for kv in range(A_KV_HEADS):
        for pair in range(2):
            qs = slice((2 * kv + pair) * LANES, (2 * kv + pair + 1) * LANES)
            o_ref[:, qs] = (outs[kv][pair * t:(pair + 1) * t] * _silu(z_ref[:, qs].astype(F32))).astype(o_ref.dtype)


def _swa(q, z, kv, pos, q_g, k_g, sinks):
    bsz, seq, d = q.shape
    t = CHUNK
    tile2 = lambda g: jnp.tile(g.astype(F32).reshape(1, A_HEAD_DIM), (1, LANES // A_HEAD_DIM))
    qi = jnp.arange(t)[:, None]
    ki = jnp.arange(t)[None, :]
    cur = jnp.where(ki <= qi, 0.0, NEG_BIG).astype(F32)
    prev = jnp.where(ki > qi, 0.0, NEG_BIG).astype(F32)
    first = jnp.concatenate([jnp.full((t, t), NEG_BIG, F32), cur], axis=1)
    later = jnp.concatenate([prev, cur], axis=1)
    bias = jnp.stack([jnp.tile(first, (1, 2)), jnp.tile(later, (1, 2))])
    lane = jnp.arange(LANES)
    in_head = lane % A_HEAD_DIM
    same_head = (lane[:, None] // A_HEAD_DIM == lane[None, :] // A_HEAD_DIM)
    partner = jnp.where(in_head < ROPE_HALF, lane + ROPE_HALF, jnp.where(in_head < ROPE_DIM, lane - ROPE_HALF, -1))
    zeros = jnp.zeros((LANES, LANES), F32)
    mean_rot = jnp.block([[same_head.astype(F32) / A_HEAD_DIM, zeros],
                          [zeros, (lane[:, None] == partner[None, :]).astype(F32)]]).astype(BF16)
    consts = [mean_rot, tile2(q_g) * (A_HEAD_DIM ** -0.5), tile2(k_g), sinks.astype(F32).reshape(1, A_HEADS)]
    act_spec = lambda width: pl.BlockSpec((None, t, width), lambda b, c: (b, c, 0))
    return pl.pallas_call(
        _swa_kernel,
        grid=(bsz, seq // t),
        in_specs=[act_spec(d), act_spec(d), act_spec(2 * A_KV), act_spec(1),
                  pl.BlockSpec((None, t, 4 * t), lambda b, c: (jnp.minimum(c, 1), 0, 0))]
        + [_resident(a, 2) for a in consts],
        out_specs=act_spec(d),
        out_shape=jax.ShapeDtypeStruct((bsz, seq, d), BF16),
        scratch_shapes=[pltpu.VMEM((A_KV_HEADS, 4 * t, LANES), BF16),
                        pltpu.VMEM((A_KV_HEADS, 4 * t, 2 * LANES), BF16)],
        compiler_params=_params("parallel", "arbitrary"),
        name="swa",
    )(q, z, kv, pos, bias, *consts)


def kernel(x, c, positions, ada_w, ada_b, norm_g, ev_w_in, s5_a_re, s5_a_im, s5_log_dt, s5_b_re, s5_b_im, s5_c_re, s5_c_im, s5_d, s5_w_glu, m_conv_w, m_conv_b, m_wq, m_wk, m_wv, m_wo, m_w_if, m_b_if, m_norm_g, m_skip, ev_w_out, od_w_in, od_q_norm_g, od_k_norm_g, od_sinks, od_w_out):
    bsz, seq, d = x.shape
    assert d == D_MODEL and seq % CHUNK == 0 and bsz * S5_PERM_T == 2 * LANES
    mod = _adaln(c, ada_w, ada_b)
    h = x.astype(F32)
    pos = positions.astype(F32)[:, :, None]
    a_q = A_HEADS * A_HEAD_DIM
    for layer in range(DEPTH):
        i = layer // 2
        if layer % 2 == 0:
            s5_u, s5_z, m_in, m_z = _inproj(h, mod, layer, norm_g[layer], ev_w_in[i], [d] * 4)
            tables = _s5_tables(s5_a_re[i], s5_a_im[i], s5_log_dt[i], s5_b_re[i], s5_b_im[i],
                                s5_c_re[i], s5_c_im[i])
            y_s5 = _s5(s5_u, s5_z, tables, s5_d[i], s5_w_glu[i])
            y_m = _mlstm(m_in, m_z, m_conv_w[i], m_conv_b[i], m_wq[i], m_wk[i], m_wv[i], m_wo[i],
                         m_w_if[i], m_b_if[i], m_norm_g[i], m_skip[i])
            h = _outproj(h, mod, layer, [y_s5, y_m], ev_w_out[i])
        else:
            w = od_w_in[i]
            w = jnp.concatenate([w[:, :a_q], w[:, a_q + 2 * A_KV:], w[:, a_q:a_q + 2 * A_KV]], axis=1)
            q, z, kv = _inproj(h, mod, layer, norm_g[layer], w, [a_q, a_q, 2 * A_KV])
            attn = _swa(q, z, kv, pos, od_q_norm_g[i], od_k_norm_g[i], od_sinks[i])
            h = _outproj(h, mod, layer, [attn], od_w_out[i])
    return h.astype(x.dtype)
```

```python
import functools
import math

import jax
import jax.numpy as jnp
from jax import lax
from jax.experimental import pallas as pl
from jax.experimental.pallas import tpu as pltpu

F32 = jnp.float32
BF16 = jnp.bfloat16
HIGHEST = lax.Precision.HIGHEST

EPS = 1e-6
D_MODEL = 1024
DEPTH = 4
S5_GROUP = 16
S5_STATE = 64
S5_GROUPS = D_MODEL // S5_GROUP
S5_GROUPS_PER_TILE = 16
S5_TILES = S5_GROUPS // S5_GROUPS_PER_TILE
S5_TILE_CH = S5_GROUPS_PER_TILE * S5_GROUP
S5_TILE_ST = S5_GROUPS_PER_TILE * S5_STATE
S5_SCAN_COLS = 256
S5_TIME_TILE = 32
S5_PERM_T = 16
M_HEADS = 4
M_HEAD_DIM = D_MODEL // M_HEADS
M_CONV = 4
CHUNK = 128
CONV_TAIL = 16
A_HEAD_DIM = 64
A_HEADS = D_MODEL // A_HEAD_DIM
A_KV_HEADS = A_HEADS // 4
A_KV = A_KV_HEADS * A_HEAD_DIM
ROPE_THETA = 500000.0
ROPE_DIM = A_HEAD_DIM // 4
ROPE_HALF = ROPE_DIM // 2
LANES = 128
NEG_BIG = -1e30

ROW_TILE = 512
VMEM_LIMIT = 56 * 1024 * 1024
NT_DIMS = (((1,), (1,)), ((), ()))


def _silu(x):
    return x * jax.nn.sigmoid(x)


def _gelu_tanh(x):
    return 0.5 * x * (1.0 + jnp.tanh(math.sqrt(2.0 / math.pi) * (x + 0.044715 * (x * x * x))))


def _params(*sem):
    return pltpu.CompilerParams(dimension_semantics=sem, vmem_limit_bytes=VMEM_LIMIT)


def _resident(arr, n_grid):
    zeros = (0,) * arr.ndim
    index_map = (lambda i: zeros) if n_grid == 1 else (lambda i, j: zeros)
    return pl.BlockSpec(arr.shape, index_map, pipeline_mode=pl.Buffered(1))


def _adaln_kernel(c_ref, w_ref, b_ref, o_ref):
    s = _silu(c_ref[...])
    o_ref[...] = jnp.dot(s, w_ref[...], preferred_element_type=F32, precision=HIGHEST) + b_ref[...]


def _adaln(c, ada_w, ada_b):
    bsz = c.shape[0]
    mod = pl.pallas_call(
        _adaln_kernel,
        grid=(DEPTH, 3),
        in_specs=[
            pl.BlockSpec((bsz, D_MODEL), lambda l, j: (0, 0)),
            pl.BlockSpec((None, D_MODEL, D_MODEL), lambda l, j: (l, 0, j)),
            pl.BlockSpec((None, None, 1, D_MODEL), lambda l, j: (l, j, 0, 0)),
        ],
        out_specs=pl.BlockSpec((None, None, bsz, D_MODEL), lambda l, j: (l, j, 0, 0)),
        out_shape=jax.ShapeDtypeStruct((DEPTH, 3, bsz, D_MODEL), F32),
        compiler_params=_params("parallel", "parallel"),
        name="adaln",
    )(c.astype(F32), ada_w.astype(F32), ada_b.astype(F32).reshape(DEPTH, 3, 1, D_MODEL))
    return jnp.transpose(mod, (0, 2, 1, 3))


def _mod_spec(layer):
    return pl.BlockSpec((None, None, 3, D_MODEL), lambda b, i: (layer, b, 0, 0))


def _inproj_kernel(h_ref, mod_ref, g_ref, w_ref, *o_refs):
    x = h_ref[...]
    ms = jnp.mean(x * x, axis=-1, keepdims=True)
    hn = x * lax.rsqrt(ms + EPS) * g_ref[...] * (1.0 + mod_ref[1:2, :]) + mod_ref[0:1, :]
    hb = hn.astype(BF16)
    c0 = 0
    for o_ref in o_refs:
        width = o_ref.shape[-1]
        o_ref[...] = jnp.dot(hb, w_ref[:, c0:c0 + width], preferred_element_type=F32).astype(o_ref.dtype)
        c0 += width


def _inproj(h, mod, layer, g, w, widths):
    bsz, seq, d = h.shape
    assert sum(widths) == w.shape[1]
    rt = min(ROW_TILE, seq)
    w = w.astype(BF16)
    return pl.pallas_call(
        _inproj_kernel,
        grid=(bsz, seq // rt),
        in_specs=[
            pl.BlockSpec((None, rt, d), lambda b, i: (b, i, 0)),
            _mod_spec(layer),
            pl.BlockSpec((1, d), lambda b, i: (0, 0)),
            _resident(w, 2),
        ],
        out_specs=[pl.BlockSpec((None, rt, wd), lambda b, i: (b, i, 0)) for wd in widths],
        out_shape=[jax.ShapeDtypeStruct((bsz, seq, wd), BF16) for wd in widths],
        compiler_params=_params("parallel", "parallel"),
        name="inproj",
    )(h, mod, g.astype(F32).reshape(1, d), w)


def _outproj_kernel(*refs, n_parts):
    h_ref, mod_ref = refs[0], refs[1]
    y_refs = refs[2:2 + n_parts]
    w_ref, o_ref = refs[2 + n_parts], refs[3 + n_parts]
    acc = None
    for i, y_ref in enumerate(y_refs):
        k = y_ref.shape[-1]
        part = jnp.dot(y_ref[...], w_ref[i * k:(i + 1) * k, :], preferred_element_type=F32)
        acc = part if acc is None else acc + part
    o_ref[...] = h_ref[...] + mod_ref[2:3, :] * acc


def _outproj(h, mod, layer, parts, w):
    bsz, seq, d = h.shape
    rt = min(ROW_TILE, seq)
    act_spec = lambda width: pl.BlockSpec((None, rt, width), lambda b, i: (b, i, 0))
    w = w.astype(BF16)
    return pl.pallas_call(
        functools.partial(_outproj_kernel, n_parts=len(parts)),
        grid=(bsz, seq // rt),
        in_specs=[act_spec(d), _mod_spec(layer)] + [act_spec(p.shape[-1]) for p in parts] + [_resident(w, 2)],
        out_specs=act_spec(d),
        out_shape=jax.ShapeDtypeStruct((bsz, seq, d), F32),
        compiler_params=_params("parallel", "parallel"),
        name="outproj",
    )(h, mod, *parts, w)


def _s5_kernel(u_ref, z_ref, perm_ref, permt_ref, bmat_ref, cmat_ref, are_ref, aim_ref, d_ref, wglu_ref,
               o_ref, st_re, st_im, u_tm, z_tm, x_re, x_im, y_acc):
    bsz, tt, d = u_ref.shape
    rows = tt * bsz
    pt = S5_PERM_T
    blk_rows = pt * bsz

    @pl.when(pl.program_id(0) == 0)
    def _():
        st_re[...] = jnp.zeros_like(st_re)
        st_im[...] = jnp.zeros_like(st_im)

    for src_ref, dst in ((u_ref, u_tm), (z_ref, z_tm)):
        for tb in range(tt // pt):
            by_seq = jnp.concatenate([src_ref[b, tb * pt:(tb + 1) * pt, :] for b in range(bsz)], axis=0)
            dst[tb * blk_rows:(tb + 1) * blk_rows, :] = jnp.dot(
                perm_ref[...], by_seq, preferred_element_type=F32).astype(BF16)

    u = u_tm[...]

    def project_in(j):
        u_j = u[:, j * S5_TILE_CH:(j + 1) * S5_TILE_CH]
        x_re[j] = jnp.dot(u_j, bmat_ref[j, :, :S5_TILE_ST], preferred_element_type=F32)
        x_im[j] = jnp.dot(u_j, bmat_ref[j, :, S5_TILE_ST:], preferred_element_type=F32)

    def recur(j):
        for c0 in range(0, S5_TILE_ST, S5_SCAN_COLS):
            cols = slice(c0, c0 + S5_SCAN_COLS)
            st_cols = slice(j * S5_TILE_ST + c0, j * S5_TILE_ST + c0 + S5_SCAN_COLS)
            a_re = jnp.broadcast_to(are_ref[j, :, cols], (bsz, S5_SCAN_COLS))
            a_im = jnp.broadcast_to(aim_ref[j, :, cols], (bsz, S5_SCAN_COLS))
            s_re, s_im = st_re[:, st_cols], st_im[:, st_cols]
            for t in range(tt):
                r = slice(t * bsz, (t + 1) * bsz)
                s_re, s_im = (a_re * s_re - a_im * s_im + x_re[j, r, cols],
                              a_re * s_im + a_im * s_re + x_im[j, r, cols])
                x_re[j, r, cols] = s_re
                x_im[j, r, cols] = s_im
            st_re[:, st_cols] = s_re
            st_im[:, st_cols] = s_im

    def project_out(j):
        y_acc[:, j * S5_TILE_CH:(j + 1) * S5_TILE_CH] = (
            jnp.dot(x_re[j].astype(BF16), cmat_ref[j, :S5_TILE_ST, :], preferred_element_type=F32)
            + jnp.dot(x_im[j].astype(BF16), cmat_ref[j, S5_TILE_ST:, :], preferred_element_type=F32))

    project_in(0)
    for j in range(S5_TILES):
        if j + 1 < S5_TILES:
            project_in(j + 1)
        recur(j)
        project_out(j)

    y = _gelu_tanh(y_acc[...] + d_ref[...] * u.astype(F32))
    lin = jnp.dot(y.astype(BF16), wglu_ref[...], preferred_element_type=F32)
    out = (y * jax.nn.sigmoid(lin) * _silu(z_tm[...].astype(F32))).astype(BF16)
    for tb in range(tt // pt):
        by_seq = jnp.dot(permt_ref[...], out[tb * blk_rows:(tb + 1) * blk_rows, :],
                         preferred_element_type=F32).astype(o_ref.dtype)
        for b in range(bsz):
            o_ref[b, tb * pt:(tb + 1) * pt, :] = by_seq[b * pt:(b + 1) * pt, :]


def _s5_tables(a_re, a_im, log_dt, b_re, b_im, c_re, c_im):
    g, n, p = S5_GROUPS, S5_STATE, S5_GROUP
    gt, nt = S5_GROUPS_PER_TILE, S5_TILES
    a = lax.complex(a_re.astype(F32), a_im.astype(F32))
    dt = jnp.exp(log_dt.astype(F32))[:, None]
    a_bar = jnp.exp(a * dt)
    b_bar = ((a_bar - 1.0) / a)[..., None] * lax.complex(b_re.astype(F32), b_im.astype(F32))
    eye = jnp.eye(gt, dtype=F32)

    def in_block(b):
        return jnp.einsum('jgnp,gh->jgphn', b.reshape(nt, gt, n, p), eye).reshape(nt, gt * p, gt * n)

    def out_block(c):
        return jnp.einsum('jgpn,gh->jgnhp', c.reshape(nt, gt, p, n), eye).reshape(nt, gt * n, gt * p)

    bmat = jnp.concatenate([in_block(b_bar.real), in_block(b_bar.imag)], axis=-1)
    cmat = jnp.concatenate([out_block(c_re.astype(F32)), out_block(-c_im.astype(F32))], axis=1)
    a_cols = lambda v: v.reshape(nt, 1, gt * n)
    return bmat.astype(BF16), cmat.astype(BF16), a_cols(a_bar.real), a_cols(a_bar.imag)


def _s5(u, z, tables, d_skip, w_glu):
    bsz, seq, d = u.shape
    tt = min(S5_TIME_TILE, seq)
    rows = tt * bsz
    bmat, cmat, a_re, a_im = tables
    blk_rows = S5_PERM_T * bsz
    perm = jnp.eye(blk_rows, dtype=BF16).reshape(bsz, S5_PERM_T, blk_rows).transpose(1, 0, 2).reshape(blk_rows, blk_rows)
    consts = [perm, perm.T, bmat, cmat, a_re, a_im, d_skip.astype(F32).reshape(1, d), w_glu.astype(BF16)]
    act_spec = pl.BlockSpec((bsz, tt, d), lambda i: (0, i, 0))
    return pl.pallas_call(
        _s5_kernel,
        grid=(seq // tt,),
        in_specs=[act_spec, act_spec] + [_resident(a, 1) for a in consts],
        out_specs=act_spec,
        out_shape=jax.ShapeDtypeStruct((bsz, seq, d), BF16),
        scratch_shapes=[
            pltpu.VMEM((bsz, S5_GROUPS * S5_STATE), F32),
            pltpu.VMEM((bsz, S5_GROUPS * S5_STATE), F32),
            pltpu.VMEM((rows, d), BF16),
            pltpu.VMEM((rows, d), BF16),
            pltpu.VMEM((S5_TILES, rows, S5_TILE_ST), F32),
            pltpu.VMEM((S5_TILES, rows, S5_TILE_ST), F32),
            pltpu.VMEM((rows, d), F32),
        ],
        compiler_params=_params("arbitrary"),
        name="s5",
    )(u, z, *consts)


def _mlstm_kernel(mi_ref, mz_ref, shift_ref, convw_ref, convb_ref, wqk_ref, wvo_ref, wif_ref, bif_ref,
                  ng_ref, skip_ref, o_ref, ext, c_st, n_st, m_st):
    t = CHUNK
    dh = M_HEAD_DIM

    @pl.when(pl.program_id(1) == 0)
    def _():
        ext[0:CONV_TAIL, :] = jnp.zeros((CONV_TAIL, D_MODEL), BF16)
        c_st[...] = jnp.zeros_like(c_st)
        n_st[...] = jnp.zeros_like(n_st)
        m_st[...] = jnp.zeros_like(m_st)

    mi_b = mi_ref[...]
    mi = mi_b.astype(F32)
    ext[CONV_TAIL:, :] = mi_b
    taps = jnp.dot(shift_ref[...], ext[...], preferred_element_type=F32)
    ext[0:CONV_TAIL, :] = mi_b[t - CONV_TAIL:, :]
    conv = convb_ref[...] + mi * convw_ref[M_CONV - 1:M_CONV, :]
    for k in range(M_CONV - 1):
        conv = conv + taps[k * t:(k + 1) * t, :] * convw_ref[k:k + 1, :]
    xc = _silu(conv)
    xc_b = xc.astype(BF16)

    gates = (jnp.dot(xc_b, wif_ref[0:D_MODEL, :], preferred_element_type=F32)
             + jnp.dot(mi_b, wif_ref[D_MODEL:, :], preferred_element_type=F32) + bif_ref[...])
    r_idx = lax.broadcasted_iota(jnp.int32, (t, t), 0)
    c_idx = lax.broadcasted_iota(jnp.int32, (t, t), 1)
    tri = r_idx >= c_idx
    cum = jnp.dot(tri.astype(F32), jax.nn.log_sigmoid(gates), preferred_element_type=F32, precision=HIGHEST)
    g_col, g_row = gates, gates.T
    b_col, b_row = cum, cum.T

    heads = range(M_HEADS)
    hsl = [slice(h * dh, (h + 1) * dh) for h in heads]
    qk = [jnp.dot(xc_b[:, hsl[h]], wqk_ref[h], preferred_element_type=F32) for h in heads]
    vo = [jnp.dot(mi_b[:, hsl[h]], wvo_ref[h], preferred_element_type=F32) for h in heads]
    q = [qk[h][:, :dh] for h in heads]
    k = [qk[h][:, dh:] for h in heads]
    q_b = [q[h].astype(BF16) for h in heads]
    v_b = [vo[h][:, :dh].astype(BF16) for h in heads]
    c_prev = [c_st[h] for h in heads]
    n_prev = [n_st[h] for h in heads]
    m_prev = [m_st[h] for h in heads]
    qkt = [lax.dot_general(q_b[h], k[h].astype(BF16), NT_DIMS, preferred_element_type=F32) for h in heads]
    qc = [jnp.dot(q_b[h], c_prev[h].astype(BF16), preferred_element_type=F32) for h in heads]

    s, g, m_t, kw, decay = [], [], [], [], []
    for h in heads:
        bc = b_col[:, M_HEADS + h:M_HEADS + h + 1]
        br = b_row[M_HEADS + h:M_HEADS + h + 1, :]
        d_log = jnp.where(tri, bc - br + g_row[h:h + 1, :], -jnp.inf)
        inter = bc + m_prev[h]
        m_t.append(jnp.maximum(inter, jnp.max(d_log, axis=-1, keepdims=True)))
        s.append(qkt[h] * jnp.exp(d_log - m_t[h]))
        g.append(jnp.exp(inter - m_t[h]))
        b_last = br[:, t - 1:t]
        w_col = b_last - bc + g_col[:, h:h + 1]
        m_new = jnp.maximum(b_last + m_prev[h], jnp.max(w_col, axis=0, keepdims=True))
        kw.append(k[h] * jnp.exp(w_col - m_new))
        decay.append(jnp.exp(b_last + m_prev[h] - m_new))
        m_st[h] = m_new

    sv = [jnp.dot(s[h].astype(BF16), v_b[h], preferred_element_type=F32) for h in heads]
    kv = [lax.dot_general(kw[h].astype(BF16), v_b[h], (((0,), (0,)), ((), ())), preferred_element_type=F32)
          for h in heads]

    mz = mz_ref[...].astype(F32)
    for h in heads:
        c_st[h] = decay[h] * c_prev[h] + kv[h]
        n_st[h] = decay[h] * n_prev[h] + jnp.sum(kw[h], axis=0, keepdims=True)
        num = sv[h] + g[h] * qc[h]
        den = jnp.sum(s[h], axis=-1, keepdims=True) + g[h] * jnp.sum(q[h] * n_prev[h], axis=-1, keepdims=True)
        inv = 1.0 / jnp.maximum(jnp.abs(den), jnp.exp(-m_t[h]))
        oh = jax.nn.sigmoid(vo[h][:, dh:]) * (num * inv)
        ms = jnp.mean(oh * oh, axis=-1, keepdims=True)
        y = oh * lax.rsqrt(ms + EPS) * ng_ref[:, hsl[h]] + skip_ref[:, hsl[h]] * xc[:, hsl[h]]
        o_ref[:, hsl[h]] = (y * _silu(mz[:, hsl[h]])).astype(o_ref.dtype)


def _mlstm(m_in, m_z, conv_w, conv_b, wq, wk, wv, wo, w_if, b_if, norm_g, skip):
    bsz, seq, d = m_in.shape
    t = CHUNK
    wqk = jnp.concatenate([wq, wk * (M_HEAD_DIM ** -0.5)], axis=-1).astype(BF16)
    wvo = jnp.concatenate([wv, wo], axis=-1).astype(BF16)
    pad = LANES - 2 * M_HEADS
    w_if = jnp.pad(w_if.astype(BF16), ((0, 0), (0, pad)))
    b_if = jnp.pad(b_if.astype(F32), (0, pad)).reshape(1, LANES)
    rows = jnp.arange((M_CONV - 1) * t)
    src = CONV_TAIL + rows % t + rows // t - (M_CONV - 1)
    shift = (src[:, None] == jnp.arange(CONV_TAIL + t)[None, :]).astype(BF16)
    consts = [shift, conv_w.astype(F32), conv_b.astype(F32).reshape(1, d), wqk, wvo, w_if, b_if,
              norm_g.astype(F32).reshape(1, d), skip.astype(F32).reshape(1, d)]
    act_spec = pl.BlockSpec((None, CHUNK, d), lambda b, c: (b, c, 0))
    return pl.pallas_call(
        _mlstm_kernel,
        grid=(bsz, seq // CHUNK),
        in_specs=[act_spec, act_spec] + [_resident(a, 2) for a in consts],
        out_specs=act_spec,
        out_shape=jax.ShapeDtypeStruct((bsz, seq, d), BF16),
        scratch_shapes=[
            pltpu.VMEM((CONV_TAIL + CHUNK, d), BF16),
            pltpu.VMEM((M_HEADS, M_HEAD_DIM, M_HEAD_DIM), F32),
            pltpu.VMEM((M_HEADS, 1, M_HEAD_DIM), F32),
            pltpu.VMEM((M_HEADS, 1, 1), F32),
        ],
        compiler_params=_params("parallel", "arbitrary"),
        name="mlstm",
    )(m_in, m_z, *consts)


def _rope_tables(pos):
    lane = lax.broadcasted_iota(jnp.int32, (1, LANES), 1)
    in_head = lane % A_HEAD_DIM
    freq = (lane % ROPE_HALF).astype(F32)
    inv_freq = jnp.exp(-math.log(ROPE_THETA) * freq / ROPE_HALF)
    ang = pos * inv_freq
    cos, sin = jnp.cos(ang), jnp.sin(ang)
    first, second = in_head < ROPE_HALF, (in_head >= ROPE_HALF) & (in_head < ROPE_DIM)
    cos_t = jnp.where(first | second, cos, 1.0)
    sin_t = jnp.where(first, -sin, jnp.where(second, sin, 0.0))
    return cos_t, sin_t


def _norm_rope(x, gain, mean_rot, tables):
    cos_t, sin_t = tables
    t = x.shape[0]
    n = x.shape[1] // LANES
    xs = jnp.concatenate([x[:, s * LANES:(s + 1) * LANES] for s in range(n)], axis=0)
    xg = xs * gain
    both = jnp.dot(jnp.concatenate([(xs * xs).astype(BF16), xg.astype(BF16)], axis=1), mean_rot,
                   preferred_element_type=F32)
    ms, partner = both[:, :LANES], both[:, LANES:]
    roped = xg.reshape(n, t, LANES) * cos_t + partner.reshape(n, t, LANES) * sin_t
    return roped.reshape(n * t, LANES) * lax.rsqrt(ms + EPS)


def _swa_kernel(q_ref, z_ref, kv_ref, pos_ref, bias_ref, mrot_ref, qg_ref, kg_ref, sink_ref, o_ref,
                k_buf, v_buf):
    t = CHUNK
    hd = A_HEAD_DIM
    blk = pl.program_id(1)
    low = lax.broadcasted_iota(jnp.int32, (1, LANES), 1) < hd

    @pl.when(blk == 0)
    def _():
        ones = jnp.broadcast_to(jnp.where(low, 1.0, 0.0), (2 * t, LANES))
        for kv in range(A_KV_HEADS):
            for part in range(2):
                k_buf[kv, 2 * part * t:(2 * part + 1) * t, :] = jnp.zeros((t, LANES), BF16)
                v_buf[kv, 2 * part * t:(2 * part + 1) * t, 0:LANES] = jnp.zeros((t, LANES), BF16)
                v_buf[kv, 2 * part * t:(2 * part + 2) * t, LANES:] = (ones if part == 0 else 1.0 - ones).astype(BF16)

    @pl.when(blk > 0)
    def _():
        for kv in range(A_KV_HEADS):
            for part in range(2):
                k_buf[kv, 2 * part * t:(2 * part + 1) * t, :] = k_buf[kv, (2 * part + 1) * t:(2 * part + 2) * t, :]
                v_buf[kv, 2 * part * t:(2 * part + 1) * t, 0:LANES] = (
                    v_buf[kv, (2 * part + 1) * t:(2 * part + 2) * t, 0:LANES])

    tables = _rope_tables(pos_ref[...])
    mean_rot = mrot_ref[...]

    kr = _norm_rope(kv_ref[:, 0:A_KV].astype(F32), kg_ref[...], mean_rot, tables)
    for s in range(A_KV // LANES):
        for src, buf in ((kr[s * t:(s + 1) * t], k_buf),
                         (kv_ref[:, A_KV + s * LANES:A_KV + (s + 1) * LANES].astype(F32), v_buf)):
            lo = jnp.where(low, src, 0.0)
            hi = jnp.where(low, 0.0, src)
            buf[2 * s, t:2 * t, 0:LANES] = lo.astype(BF16)
            buf[2 * s, 3 * t:4 * t, 0:LANES] = pltpu.roll(lo, hd, 1).astype(BF16)
            buf[2 * s + 1, t:2 * t, 0:LANES] = pltpu.roll(hi, hd, 1).astype(BF16)
            buf[2 * s + 1, 3 * t:4 * t, 0:LANES] = hi.astype(BF16)

    qr = _norm_rope(q_ref[...].astype(F32), qg_ref[...], mean_rot, tables).astype(BF16)
    bias = bias_ref[...]
    upper = lax.broadcasted_iota(jnp.int32, (2 * t, 1), 0) < t
    scores = [lax.dot_general(qr[2 * kv * t:(2 * kv + 2) * t], k_buf[kv], NT_DIMS, preferred_element_type=F32)
              for kv in range(A_KV_HEADS)]
    outs = []
    for kv in range(A_KV_HEADS):
        sc = (scores[kv].reshape(2, t, 4 * t) + bias).reshape(2 * t, 4 * t)
        probs, sink_terms = [], []
        for half in range(2):
            sink = jnp.where(upper, sink_ref[4 * kv + half:4 * kv + half + 1, :],
                             sink_ref[4 * kv + 2 + half:4 * kv + 3 + half, :])
            sc_h = [sc[:, (2 * half + c) * t:(2 * half + c + 1) * t] for c in range(2)]
            m = jnp.maximum(jnp.max(jnp.maximum(sc_h[0], sc_h[1]), axis=-1, keepdims=True), sink)
            probs += [jnp.exp(sc_h[c] - m).astype(BF16) for c in range(2)]
            sink_terms.append(jnp.exp(sink - m))
        o = jnp.dot(jnp.concatenate(probs, axis=1), v_buf[kv], preferred_element_type=F32)
        outs.append(o[:, :LANES] / (o[:, LANES:] + jnp.where(low, sink_terms[0], sink_terms[1])))
    for kv in range(A_KV_HEADS):
        for pair in range(2):
            qs = slice((2 * kv + pair) * LANES, (2 * kv + pair + 1) * LANES)
            o_ref[:, qs] = (outs[kv][pair * t:(pair + 1) * t] * _silu(z_ref[:, qs].astype(F32))).astype(o_ref.dtype)


def _swa(q, z, kv, pos, q_g, k_g, sinks):
    bsz, seq, d = q.shape
    t = CHUNK
    tile2 = lambda g: jnp.tile(g.astype(F32).reshape(1, A_HEAD_DIM), (1, LANES // A_HEAD_DIM))
    qi = jnp.arange(t)[:, None]
    ki = jnp.arange(t)[None, :]
    cur = jnp.where(ki <= qi, 0.0, NEG_BIG).astype(F32)
    prev = jnp.where(ki > qi, 0.0, NEG_BIG).astype(F32)
    first = jnp.concatenate([jnp.full((t, t), NEG_BIG, F32), cur], axis=1)
    later = jnp.concatenate([prev, cur], axis=1)
    bias = jnp.stack([jnp.tile(first, (1, 2)), jnp.tile(later, (1, 2))])
    lane = jnp.arange(LANES)
    in_head = lane % A_HEAD_DIM
    same_head = (lane[:, None] // A_HEAD_DIM == lane[None, :] // A_HEAD_DIM)
    partner = jnp.where(in_head < ROPE_HALF, lane + ROPE_HALF, jnp.where(in_head < ROPE_DIM, lane - ROPE_HALF, -1))
    zeros = jnp.zeros((LANES, LANES), F32)
    mean_rot = jnp.block([[same_head.astype(F32) / A_HEAD_DIM, zeros],
                          [zeros, (lane[:, None] == partner[None, :]).astype(F32)]]).astype(BF16)
    sink_rows = jnp.broadcast_to(sinks.astype(F32).reshape(A_HEADS, 1), (A_HEADS, LANES))
    consts = [mean_rot, tile2(q_g) * (A_HEAD_DIM ** -0.5), tile2(k_g), sink_rows]
    act_spec = lambda width: pl.BlockSpec((None, t, width), lambda b, c: (b, c, 0))
    return pl.pallas_call(
        _swa_kernel,
        grid=(bsz, seq // t),
        in_specs=[act_spec(d), act_spec(d), act_spec(2 * A_KV), act_spec(1),
                  pl.BlockSpec((None, t, 4 * t), lambda b, c: (jnp.minimum(c, 1), 0, 0))]
        + [_resident(a, 2) for a in consts],
        out_specs=act_spec(d),
        out_shape=jax.ShapeDtypeStruct((bsz, seq, d), BF16),
        scratch_shapes=[pltpu.VMEM((A_KV_HEADS, 4 * t, LANES), BF16),
                        pltpu.VMEM((A_KV_HEADS, 4 * t, 2 * LANES), BF16)],
        compiler_params=_params("parallel", "arbitrary"),
        name="swa",
    )(q, z, kv, pos, bias, *consts)


def kernel(x, c, positions, ada_w, ada_b, norm_g, ev_w_in, s5_a_re, s5_a_im, s5_log_dt, s5_b_re, s5_b_im, s5_c_re, s5_c_im, s5_d, s5_w_glu, m_conv_w, m_conv_b, m_wq, m_wk, m_wv, m_wo, m_w_if, m_b_if, m_norm_g, m_skip, ev_w_out, od_w_in, od_q_norm_g, od_k_norm_g, od_sinks, od_w_out):
    bsz, seq, d = x.shape
    assert d == D_MODEL and seq % CHUNK == 0 and bsz * S5_PERM_T == 2 * LANES
    mod = _adaln(c, ada_w, ada_b)
    h = x.astype(F32)
    pos = positions.astype(F32)[:, :, None]
    a_q = A_HEADS * A_HEAD_DIM
    for layer in range(DEPTH):
        i = layer // 2
        if layer % 2 == 0:
            s5_u, s5_z, m_in, m_z = _inproj(h, mod, layer, norm_g[layer], ev_w_in[i], [d] * 4)
            tables = _s5_tables(s5_a_re[i], s5_a_im[i], s5_log_dt[i], s5_b_re[i], s5_b_im[i],
                                s5_c_re[i], s5_c_im[i])
            y_s5 = _s5(s5_u, s5_z, tables, s5_d[i], s5_w_glu[i])
            y_m = _mlstm(m_in, m_z, m_conv_w[i], m_conv_b[i], m_wq[i], m_wk[i], m_wv[i], m_wo[i],
                         m_w_if[i], m_b_if[i], m_norm_g[i], m_skip[i])
            h = _outproj(h, mod, layer, [y_s5, y_m], ev_w_out[i])
        else:
            w = od_w_in[i]
            w = jnp.concatenate([w[:, :a_q], w[:, a_q + 2 * A_KV:], w[:, a_q:a_q + 2 * A_KV]], axis=1)
            q, z, kv = _inproj(h, mod, layer, norm_g[layer], w, [a_q, a_q, 2 * A_KV])
            attn = _swa(q, z, kv, pos, od_q_norm_g[i], od_k_norm_g[i], od_sinks[i])
            h = _outproj(h, mod, layer, [attn], od_w_out[i])
    return h.astype(x.dtype)
```

```python
import functools
import math

import jax
import jax.numpy as jnp
from jax import lax
from jax.experimental import pallas as pl
from jax.experimental.pallas import tpu as pltpu

F32 = jnp.float32
BF16 = jnp.bfloat16
HIGHEST = lax.Precision.HIGHEST

EPS = 1e-6
D_MODEL = 1024
DEPTH = 4
S5_GROUP = 16
S5_STATE = 64
S5_GROUPS = D_MODEL // S5_GROUP
S5_GROUPS_PER_TILE = 16
S5_TILES = S5_GROUPS // S5_GROUPS_PER_TILE
S5_TILE_CH = S5_GROUPS_PER_TILE * S5_GROUP
S5_TILE_ST = S5_GROUPS_PER_TILE * S5_STATE
S5_SCAN_COLS = 256
S5_TIME_TILE = 64
S5_PERM_T = 16
M_HEADS = 4
M_HEAD_DIM = D_MODEL // M_HEADS
M_CONV = 4
CHUNK = 128
CONV_TAIL = 16
A_HEAD_DIM = 64
A_HEADS = D_MODEL // A_HEAD_DIM
A_KV_HEADS = A_HEADS // 4
A_KV = A_KV_HEADS * A_HEAD_DIM
ROPE_THETA = 500000.0
ROPE_DIM = A_HEAD_DIM // 4
ROPE_HALF = ROPE_DIM // 2
LANES = 128
NEG_BIG = -1e30

ROW_TILE = 512
SEQS_PER_STEP = 2
VMEM_LIMIT = 56 * 1024 * 1024
NT_DIMS = (((1,), (1,)), ((), ()))


_sigmoid = jax.nn.sigmoid


def _silu(x):
    return x * _sigmoid(x)


def _gelu_tanh(x):
    return 0.5 * x * (1.0 + jnp.tanh(math.sqrt(2.0 / math.pi) * (x + 0.044715 * (x * x * x))))


def _params(*sem):
    return pltpu.CompilerParams(dimension_semantics=sem, vmem_limit_bytes=VMEM_LIMIT)


def _resident(arr, n_grid):
    zeros = (0,) * arr.ndim
    index_map = (lambda i: zeros) if n_grid == 1 else (lambda i, j: zeros)
    return pl.BlockSpec(arr.shape, index_map, pipeline_mode=pl.Buffered(1))


def _adaln_kernel(c_ref, w_ref, b_ref, o_ref):
    s = _silu(c_ref[...])
    o_ref[...] = jnp.dot(s, w_ref[...], preferred_element_type=F32, precision=HIGHEST) + b_ref[...]


def _adaln(c, ada_w, ada_b):
    bsz = c.shape[0]
    mod = pl.pallas_call(
        _adaln_kernel,
        grid=(DEPTH, 3),
        in_specs=[
            pl.BlockSpec((bsz, D_MODEL), lambda l, j: (0, 0)),
            pl.BlockSpec((None, D_MODEL, D_MODEL), lambda l, j: (l, 0, j)),
            pl.BlockSpec((None, None, 1, D_MODEL), lambda l, j: (l, j, 0, 0)),
        ],
        out_specs=pl.BlockSpec((None, None, bsz, D_MODEL), lambda l, j: (l, j, 0, 0)),
        out_shape=jax.ShapeDtypeStruct((DEPTH, 3, bsz, D_MODEL), F32),
        compiler_params=_params("parallel", "parallel"),
        name="adaln",
    )(c.astype(F32), ada_w.astype(F32), ada_b.astype(F32).reshape(DEPTH, 3, 1, D_MODEL))
    return jnp.transpose(mod, (0, 2, 1, 3))


def _mod_spec(layer):
    return pl.BlockSpec((None, None, 3, D_MODEL), lambda b, i: (layer, b, 0, 0))


def _boundary_kernel(*refs, n_parts, n_outs):
    refs = list(refs)
    h_ref = refs.pop(0)
    if n_parts:
        mod_out_ref = refs.pop(0)
        y_refs = [refs.pop(0) for _ in range(n_parts)]
        w_out_ref = refs.pop(0)
    if n_outs:
        mod_in_ref, g_ref, w_in_ref = refs.pop(0), refs.pop(0), refs.pop(0)
    h = h_ref[...]
    if n_parts:
        acc = None
        for i, y_ref in enumerate(y_refs):
            k = y_ref.shape[-1]
            part = jnp.dot(y_ref[...], w_out_ref[i * k:(i + 1) * k, :], preferred_element_type=F32)
            acc = part if acc is None else acc + part
        h = h + mod_out_ref[2:3, :] * acc
        refs.pop(0)[...] = h
    if n_outs:
        ms = jnp.mean(h * h, axis=-1, keepdims=True)
        hn = h * lax.rsqrt(ms + EPS) * g_ref[...] * (1.0 + mod_in_ref[1:2, :]) + mod_in_ref[0:1, :]
        hb = hn.astype(BF16)
        c0 = 0
        for o_ref in refs:
            width = o_ref.shape[-1]
            o_ref[...] = jnp.dot(hb, w_in_ref[:, c0:c0 + width], preferred_element_type=F32).astype(o_ref.dtype)
            c0 += width


def _boundary(h, mod, close=None, open_=None):
    bsz, seq, d = h.shape
    rt = min(ROW_TILE, seq)
    act_spec = lambda width: pl.BlockSpec((None, rt, width), lambda b, i: (b, i, 0))
    args, in_specs, out_specs, out_shape = [h], [act_spec(d)], [], []
    n_parts = n_outs = 0
    if close is not None:
        layer, parts, w_out = close
        n_parts = len(parts)
        w_out = w_out.astype(BF16)
        args += [mod, *parts, w_out]
        in_specs += [_mod_spec(layer)] + [act_spec(p.shape[-1]) for p in parts] + [_resident(w_out, 2)]
        out_specs.append(act_spec(d))
        out_shape.append(jax.ShapeDtypeStruct((bsz, seq, d), F32))
    if open_ is not None:
        layer, gain, w_in, widths = open_
        assert sum(widths) == w_in.shape[1]
        n_outs = len(widths)
        w_in = w_in.astype(BF16)
        args += [mod, gain.astype(F32).reshape(1, d), w_in]
        in_specs += [_mod_spec(layer), pl.BlockSpec((1, d), lambda b, i: (0, 0)), _resident(w_in, 2)]
        out_specs += [act_spec(wd) for wd in widths]
        out_shape += [jax.ShapeDtypeStruct((bsz, seq, wd), BF16) for wd in widths]
    return pl.pallas_call(
        functools.partial(_boundary_kernel, n_parts=n_parts, n_outs=n_outs),
        grid=(bsz, seq // rt),
        in_specs=in_specs,
        out_specs=out_specs,
        out_shape=out_shape,
        compiler_params=_params("parallel", "parallel"),
        name="boundary",
    )(*args)


def _s5_kernel(u_ref, z_ref, perm_ref, permt_ref, bmat_ref, cmat_ref, are_ref, aim_ref, d_ref, wglu_ref,
               o_ref, st_re, st_im, u_tm, z_tm, x_re, x_im, y_acc):
    bsz, tt, d = u_ref.shape
    rows = tt * bsz
    pt = S5_PERM_T
    blk_rows = pt * bsz

    @pl.when(pl.program_id(0) == 0)
    def _():
        st_re[...] = jnp.zeros_like(st_re)
        st_im[...] = jnp.zeros_like(st_im)

    for src_ref, dst in ((u_ref, u_tm), (z_ref, z_tm)):
        for tb in range(tt // pt):
            by_seq = jnp.concatenate([src_ref[b, tb * pt:(tb + 1) * pt, :] for b in range(bsz)], axis=0)
            dst[tb * blk_rows:(tb + 1) * blk_rows, :] = jnp.dot(
                perm_ref[...], by_seq, preferred_element_type=F32).astype(BF16)

    u = u_tm[...]

    def project_in(j):
        u_j = u[:, j * S5_TILE_CH:(j + 1) * S5_TILE_CH]
        x_re[j] = jnp.dot(u_j, bmat_ref[j, :, :S5_TILE_ST], preferred_element_type=F32)
        x_im[j] = jnp.dot(u_j, bmat_ref[j, :, S5_TILE_ST:], preferred_element_type=F32)

    def recur(j):
        for c0 in range(0, S5_TILE_ST, S5_SCAN_COLS):
            cols = slice(c0, c0 + S5_SCAN_COLS)
            st_cols = slice(j * S5_TILE_ST + c0, j * S5_TILE_ST + c0 + S5_SCAN_COLS)
            a_re = jnp.broadcast_to(are_ref[j, :, cols], (bsz, S5_SCAN_COLS))
            a_im = jnp.broadcast_to(aim_ref[j, :, cols], (bsz, S5_SCAN_COLS))
            s_re, s_im = st_re[:, st_cols], st_im[:, st_cols]
            for t in range(tt):
                r = slice(t * bsz, (t + 1) * bsz)
                s_re, s_im = (a_re * s_re - a_im * s_im + x_re[j, r, cols],
                              a_re * s_im + a_im * s_re + x_im[j, r, cols])
                x_re[j, r, cols] = s_re
                x_im[j, r, cols] = s_im
            st_re[:, st_cols] = s_re
            st_im[:, st_cols] = s_im

    def project_out(j):
        y_acc[:, j * S5_TILE_CH:(j + 1) * S5_TILE_CH] = (
            jnp.dot(x_re[j].astype(BF16), cmat_ref[j, :S5_TILE_ST, :], preferred_element_type=F32)
            + jnp.dot(x_im[j].astype(BF16), cmat_ref[j, S5_TILE_ST:, :], preferred_element_type=F32))

    project_in(0)
    for j in range(S5_TILES):
        if j + 1 < S5_TILES:
            project_in(j + 1)
        recur(j)
        project_out(j)

    y = _gelu_tanh(y_acc[...] + d_ref[...] * u.astype(F32))
    lin = jnp.dot(y.astype(BF16), wglu_ref[...], preferred_element_type=F32)
    out = (y * _sigmoid(lin) * _silu(z_tm[...].astype(F32))).astype(BF16)
    for tb in range(tt // pt):
        by_seq = jnp.dot(permt_ref[...], out[tb * blk_rows:(tb + 1) * blk_rows, :],
                         preferred_element_type=F32).astype(o_ref.dtype)
        for b in range(bsz):
            o_ref[b, tb * pt:(tb + 1) * pt, :] = by_seq[b * pt:(b + 1) * pt, :]


def _s5_tables(a_re, a_im, log_dt, b_re, b_im, c_re, c_im):
    g, n, p = S5_GROUPS, S5_STATE, S5_GROUP
    gt, nt = S5_GROUPS_PER_TILE, S5_TILES
    a = lax.complex(a_re.astype(F32), a_im.astype(F32))
    dt = jnp.exp(log_dt.astype(F32))[:, None]
    a_bar = jnp.exp(a * dt)
    b_bar = ((a_bar - 1.0) / a)[..., None] * lax.complex(b_re.astype(F32), b_im.astype(F32))
    eye = jnp.eye(gt, dtype=F32)

    def in_block(b):
        return jnp.einsum('jgnp,gh->jgphn', b.reshape(nt, gt, n, p), eye).reshape(nt, gt * p, gt * n)

    def out_block(c):
        return jnp.einsum('jgpn,gh->jgnhp', c.reshape(nt, gt, p, n), eye).reshape(nt, gt * n, gt * p)

    bmat = jnp.concatenate([in_block(b_bar.real), in_block(b_bar.imag)], axis=-1)
    cmat = jnp.concatenate([out_block(c_re.astype(F32)), out_block(-c_im.astype(F32))], axis=1)
    a_cols = lambda v: v.reshape(nt, 1, gt * n)
    return bmat.astype(BF16), cmat.astype(BF16), a_cols(a_bar.real), a_cols(a_bar.imag)


def _s5(u, z, tables, d_skip, w_glu):
    bsz, seq, d = u.shape
    tt = min(S5_TIME_TILE, seq)
    rows = tt * bsz
    bmat, cmat, a_re, a_im = tables
    blk_rows = S5_PERM_T * bsz
    perm = jnp.eye(blk_rows, dtype=BF16).reshape(bsz, S5_PERM_T, blk_rows).transpose(1, 0, 2).reshape(blk_rows, blk_rows)
    consts = [perm, perm.T, bmat, cmat, a_re, a_im, d_skip.astype(F32).reshape(1, d), w_glu.astype(BF16)]
    act_spec = pl.BlockSpec((bsz, tt, d), lambda i: (0, i, 0))
    return pl.pallas_call(
        _s5_kernel,
        grid=(seq // tt,),
        in_specs=[act_spec, act_spec] + [_resident(a, 1) for a in consts],
        out_specs=act_spec,
        out_shape=jax.ShapeDtypeStruct((bsz, seq, d), BF16),
        scratch_shapes=[
            pltpu.VMEM((bsz, S5_GROUPS * S5_STATE), F32),
            pltpu.VMEM((bsz, S5_GROUPS * S5_STATE), F32),
            pltpu.VMEM((rows, d), BF16),
            pltpu.VMEM((rows, d), BF16),
            pltpu.VMEM((S5_TILES, rows, S5_TILE_ST), F32),
            pltpu.VMEM((S5_TILES, rows, S5_TILE_ST), F32),
            pltpu.VMEM((rows, d), F32),
        ],
        compiler_params=_params("arbitrary"),
        name="s5",
    )(u, z, *consts)


def _mlstm_kernel(mi_ref, mz_ref, shift_ref, convw_ref, convb_ref, wqk_ref, wvo_ref, wif_ref, bif_ref,
                  ng_ref, skip_ref, o_ref, ext, c_st, n_st, m_st):
    t = CHUNK
    dh = M_HEAD_DIM
    seqs = range(mi_ref.shape[0])
    heads = range(M_HEADS)
    units = [(sq, h) for sq in seqs for h in heads]
    hsl = [slice(h * dh, (h + 1) * dh) for h in heads]
    slot = lambda sq, h: sq * M_HEADS + h

    @pl.when(pl.program_id(1) == 0)
    def _():
        for sq in seqs:
            ext[sq, 0:CONV_TAIL, :] = jnp.zeros((CONV_TAIL, D_MODEL), BF16)
        c_st[...] = jnp.zeros_like(c_st)
        n_st[...] = jnp.zeros_like(n_st)
        m_st[...] = jnp.zeros_like(m_st)

    tri = lax.broadcasted_iota(jnp.int32, (t, t), 0) >= lax.broadcasted_iota(jnp.int32, (t, t), 1)
    tri_f = tri.astype(F32)

    def project(sq, f):
        mi_b = mi_ref[sq]
        ext[sq, CONV_TAIL:, :] = mi_b
        taps = jnp.dot(shift_ref[...], ext[sq], preferred_element_type=F32)
        ext[sq, 0:CONV_TAIL, :] = mi_b[t - CONV_TAIL:, :]
        f["vo"] = [jnp.dot(mi_b[:, hsl[h]], wvo_ref[h], preferred_element_type=F32) for h in heads]
        yield
        conv = convb_ref[...] + mi_b.astype(F32) * convw_ref[M_CONV - 1:M_CONV, :]
        for k in range(M_CONV - 1):
            conv = conv + taps[k * t:(k + 1) * t, :] * convw_ref[k:k + 1, :]
        f["xc"] = _silu(conv)
        xc_b = f["xc"].astype(BF16)
        f["gates"] = (jnp.dot(xc_b, wif_ref[0:D_MODEL, :], preferred_element_type=F32)
                      + jnp.dot(mi_b, wif_ref[D_MODEL:, :], preferred_element_type=F32) + bif_ref[...])
        qk = [jnp.dot(xc_b[:, hsl[h]], wqk_ref[h], preferred_element_type=F32) for h in heads]
        yield
        f["cum"] = jnp.dot(tri_f, jax.nn.log_sigmoid(f["gates"]), preferred_element_type=F32, precision=HIGHEST)
        f["q"] = [qk[h][:, :dh] for h in heads]
        f["k"] = [qk[h][:, dh:] for h in heads]
        q_b = [f["q"][h].astype(BF16) for h in heads]
        f["c_prev"] = [c_st[slot(sq, h)] for h in heads]
        f["qkt"] = [lax.dot_general(q_b[h], f["k"][h].astype(BF16), NT_DIMS, preferred_element_type=F32)
                    for h in heads]
        f["qc"] = [jnp.dot(q_b[h], f["c_prev"][h].astype(BF16), preferred_element_type=F32) for h in heads]
        yield
        f["g_row"], f["b_row"] = f["gates"].T, f["cum"].T

    def cell(sq, f):
        st = [slot(sq, h) for h in heads]
        n_prev = [n_st[st[h]] for h in heads]
        m_prev = [m_st[st[h]] for h in heads]
        v_b = [f["vo"][h][:, :dh].astype(BF16) for h in heads]
        s, g, m_t, kw, decay = [], [], [], [], []
        for h in heads:
            bc = f["cum"][:, M_HEADS + h:M_HEADS + h + 1]
            br = f["b_row"][M_HEADS + h:M_HEADS + h + 1, :]
            d_log = jnp.where(tri, bc - br + f["g_row"][h:h + 1, :], -jnp.inf)
            inter = bc + m_prev[h]
            m_t.append(jnp.maximum(inter, jnp.max(d_log, axis=-1, keepdims=True)))
            s.append(f["qkt"][h] * jnp.exp(d_log - m_t[h]))
            g.append(jnp.exp(inter - m_t[h]))
            b_last = br[:, t - 1:t]
            w_col = b_last - bc + f["gates"][:, h:h + 1]
            m_new = jnp.maximum(b_last + m_prev[h], jnp.max(w_col, axis=0, keepdims=True))
            kw.append(f["k"][h] * jnp.exp(w_col - m_new))
            decay.append(jnp.exp(b_last + m_prev[h] - m_new))
            m_st[st[h]] = m_new
            yield
        sv = [jnp.dot(s[h].astype(BF16), v_b[h], preferred_element_type=F32) for h in heads]
        kv = [lax.dot_general(kw[h].astype(BF16), v_b[h], (((0,), (0,)), ((), ())), preferred_element_type=F32)
              for h in heads]
        yield
        for h in heads:
            c_st[st[h]] = decay[h] * f["c_prev"][h] + kv[h]
            n_st[st[h]] = decay[h] * n_prev[h] + jnp.sum(kw[h], axis=0, keepdims=True)
            num = sv[h] + g[h] * f["qc"][h]
            den = (jnp.sum(s[h], axis=-1, keepdims=True)
                   + g[h] * jnp.sum(f["q"][h] * n_prev[h], axis=-1, keepdims=True))
            inv = 1.0 / jnp.maximum(jnp.abs(den), jnp.exp(-m_t[h]))
            oh = _sigmoid(f["vo"][h][:, dh:]) * (num * inv)
            ms = jnp.mean(oh * oh, axis=-1, keepdims=True)
            y = oh * lax.rsqrt(ms + EPS) * ng_ref[:, hsl[h]] + skip_ref[:, hsl[h]] * f["xc"][:, hsl[h]]
            o_ref[sq, :, hsl[h]] = (y * _silu(mz_ref[sq, :, hsl[h]].astype(F32))).astype(o_ref.dtype)
            yield

    def alternate(*gens):
        live = list(gens)
        while live:
            for gen in list(live):
                if next(gen, "done") == "done":
                    live.remove(gen)

    fronts = [{} for _ in seqs]
    alternate(project(0, fronts[0]))
    for sq in seqs[1:]:
        alternate(project(sq, fronts[sq]), cell(sq - 1, fronts[sq - 1]))
    alternate(cell(seqs[-1], fronts[seqs[-1]]))


def _mlstm(m_in, m_z, conv_w, conv_b, wq, wk, wv, wo, w_if, b_if, norm_g, skip):
    bsz, seq, d = m_in.shape
    t = CHUNK
    wqk = jnp.concatenate([wq, wk * (M_HEAD_DIM ** -0.5)], axis=-1).astype(BF16)
    wvo = jnp.concatenate([wv, wo], axis=-1).astype(BF16)
    pad = LANES - 2 * M_HEADS
    w_if = jnp.pad(w_if.astype(BF16), ((0, 0), (0, pad)))
    b_if = jnp.pad(b_if.astype(F32), (0, pad)).reshape(1, LANES)
    rows = jnp.arange((M_CONV - 1) * t)
    src = CONV_TAIL + rows % t + rows // t - (M_CONV - 1)
    shift = (src[:, None] == jnp.arange(CONV_TAIL + t)[None, :]).astype(BF16)
    consts = [shift, conv_w.astype(F32), conv_b.astype(F32).reshape(1, d), wqk, wvo, w_if, b_if,
              norm_g.astype(F32).reshape(1, d), skip.astype(F32).reshape(1, d)]
    nb = SEQS_PER_STEP
    act_spec = pl.BlockSpec((nb, CHUNK, d), lambda b, c: (b, c, 0))
    return pl.pallas_call(
        _mlstm_kernel,
        grid=(bsz // nb, seq // CHUNK),
        in_specs=[act_spec, act_spec] + [_resident(a, 2) for a in consts],
        out_specs=act_spec,
        out_shape=jax.ShapeDtypeStruct((bsz, seq, d), BF16),
        scratch_shapes=[
            pltpu.VMEM((nb, CONV_TAIL + CHUNK, d), BF16),
            pltpu.VMEM((nb * M_HEADS, M_HEAD_DIM, M_HEAD_DIM), F32),
            pltpu.VMEM((nb * M_HEADS, 1, M_HEAD_DIM), F32),
            pltpu.VMEM((nb * M_HEADS, 1, 1), F32),
        ],
        compiler_params=_params("parallel", "arbitrary"),
        name="mlstm",
    )(m_in, m_z, *consts)


def _rope_tables(pos):
    lane = lax.broadcasted_iota(jnp.int32, (1, LANES), 1)
    in_head = lane % A_HEAD_DIM
    freq = (lane % ROPE_HALF).astype(F32)
    inv_freq = jnp.exp(-math.log(ROPE_THETA) * freq / ROPE_HALF)
    ang = pos * inv_freq
    cos, sin = jnp.cos(ang), jnp.sin(ang)
    first, second = in_head < ROPE_HALF, (in_head >= ROPE_HALF) & (in_head < ROPE_DIM)
    cos_t = jnp.where(first | second, cos, 1.0)
    sin_t = jnp.where(first, -sin, jnp.where(second, sin, 0.0))
    return cos_t, sin_t


def _norm_rope(x, gain, mean_rot, tables):
    cos_t, sin_t = tables
    t = x.shape[0]
    n = x.shape[1] // LANES
    xs = jnp.concatenate([x[:, s * LANES:(s + 1) * LANES] for s in range(n)], axis=0)
    xg = xs * gain
    both = jnp.dot(jnp.concatenate([(xs * xs).astype(BF16), xg.astype(BF16)], axis=1), mean_rot,
                   preferred_element_type=F32)
    ms, partner = both[:, :LANES], both[:, LANES:]
    roped = xg.reshape(n, t, LANES) * cos_t + partner.reshape(n, t, LANES) * sin_t
    return roped.reshape(n * t, LANES) * lax.rsqrt(ms + EPS)


def _swa_kernel(q_ref, z_ref, kv_ref, pos_ref, bias_ref, mrot_ref, qg_ref, kg_ref, sink_ref, o_ref,
                k_buf, v_buf):
    t = CHUNK
    hd = A_HEAD_DIM
    blk = pl.program_id(1)
    n_seq = q_ref.shape[0]
    low = lax.broadcasted_iota(jnp.int32, (1, LANES), 1) < hd

    @pl.when(blk == 0)
    def _():
        ones = jnp.broadcast_to(jnp.where(low, 1.0, 0.0), (2 * t, LANES))
        for kv in range(n_seq * A_KV_HEADS):
            for part in range(2):
                k_buf[kv, 2 * part * t:(2 * part + 1) * t, :] = jnp.zeros((t, LANES), BF16)
                v_buf[kv, 2 * part * t:(2 * part + 1) * t, 0:LANES] = jnp.zeros((t, LANES), BF16)
                v_buf[kv, 2 * part * t:(2 * part + 2) * t, LANES:] = (ones if part == 0 else 1.0 - ones).astype(BF16)

    @pl.when(blk > 0)
    def _():
        for kv in range(n_seq * A_KV_HEADS):
            for part in range(2):
                k_buf[kv, 2 * part * t:(2 * part + 1) * t, :] = k_buf[kv, (2 * part + 1) * t:(2 * part + 2) * t, :]
                v_buf[kv, 2 * part * t:(2 * part + 1) * t, 0:LANES] = (
                    v_buf[kv, (2 * part + 1) * t:(2 * part + 2) * t, 0:LANES])

    mean_rot = mrot_ref[...]
    bias = bias_ref[...]
    upper = lax.broadcasted_iota(jnp.int32, (2 * t, 1), 0) < t
    for sq in range(n_seq):
        tables = _rope_tables(pos_ref[sq])
        base = sq * A_KV_HEADS
        kr = _norm_rope(kv_ref[sq, :, 0:A_KV].astype(F32), kg_ref[...], mean_rot, tables)
        for s in range(A_KV // LANES):
            for src, buf in ((kr[s * t:(s + 1) * t], k_buf),
                             (kv_ref[sq, :, A_KV + s * LANES:A_KV + (s + 1) * LANES].astype(F32), v_buf)):
                lo = jnp.where(low, src, 0.0)
                hi = jnp.where(low, 0.0, src)
                buf[base + 2 * s, t:2 * t, 0:LANES] = lo.astype(BF16)
                buf[base + 2 * s, 3 * t:4 * t, 0:LANES] = pltpu.roll(lo, hd, 1).astype(BF16)
                buf[base + 2 * s + 1, t:2 * t, 0:LANES] = pltpu.roll(hi, hd, 1).astype(BF16)
                buf[base + 2 * s + 1, 3 * t:4 * t, 0:LANES] = hi.astype(BF16)

        qr = _norm_rope(q_ref[sq].astype(F32), qg_ref[...], mean_rot, tables).astype(BF16)
        scores = [lax.dot_general(qr[2 * kv * t:(2 * kv + 2) * t], k_buf[base + kv], NT_DIMS,
                                  preferred_element_type=F32)
                  for kv in range(A_KV_HEADS)]
        outs = []
        for kv in range(A_KV_HEADS):
            sc = (scores[kv].reshape(2, t, 4 * t) + bias).reshape(2 * t, 4 * t)
            probs, sink_terms = [], []
            for half in range(2):
                sink = jnp.where(upper, sink_ref[4 * kv + half:4 * kv + half + 1, :],
                                 sink_ref[4 * kv + 2 + half:4 * kv + 3 + half, :])
                sc_h = [sc[:, (2 * half + c) * t:(2 * half + c + 1) * t] for c in range(2)]
                m = jnp.maximum(jnp.max(jnp.maximum(sc_h[0], sc_h[1]), axis=-1, keepdims=True), sink)
                probs += [jnp.exp(sc_h[c] - m).astype(BF16) for c in range(2)]
                sink_terms.append(jnp.exp(sink - m))
            o = jnp.dot(jnp.concatenate(probs, axis=1), v_buf[base + kv], preferred_element_type=F32)
            outs.append(o[:, :LANES] / (o[:, LANES:] + jnp.where(low, sink_terms[0], sink_terms[1])))
        for kv in range(A_KV_HEADS):
            for pair in range(2):
                qs = slice((2 * kv + pair) * LANES, (2 * kv + pair + 1) * LANES)
                o_ref[sq, :, qs] = (outs[kv][pair * t:(pair + 1) * t]
                                    * _silu(z_ref[sq, :, qs].astype(F32))).astype(o_ref.dtype)


def _swa(q, z, kv, pos, q_g, k_g, sinks):
    bsz, seq, d = q.shape
    t = CHUNK
    tile2 = lambda g: jnp.tile(g.astype(F32).reshape(1, A_HEAD_DIM), (1, LANES // A_HEAD_DIM))
    qi = jnp.arange(t)[:, None]
    ki = jnp.arange(t)[None, :]
    cur = jnp.where(ki <= qi, 0.0, NEG_BIG).astype(F32)
    prev = jnp.where(ki > qi, 0.0, NEG_BIG).astype(F32)
    first = jnp.concatenate([jnp.full((t, t), NEG_BIG, F32), cur], axis=1)
    later = jnp.concatenate([prev, cur], axis=1)
    bias = jnp.stack([jnp.tile(first, (1, 2)), jnp.tile(later, (1, 2))])
    lane = jnp.arange(LANES)
    in_head = lane % A_HEAD_DIM
    same_head = (lane[:, None] // A_HEAD_DIM == lane[None, :] // A_HEAD_DIM)
    partner = jnp.where(in_head < ROPE_HALF, lane + ROPE_HALF, jnp.where(in_head < ROPE_DIM, lane - ROPE_HALF, -1))
    zeros = jnp.zeros((LANES, LANES), F32)
    mean_rot = jnp.block([[same_head.astype(F32) / A_HEAD_DIM, zeros],
                          [zeros, (lane[:, None] == partner[None, :]).astype(F32)]]).astype(BF16)
    sink_rows = jnp.broadcast_to(sinks.astype(F32).reshape(A_HEADS, 1), (A_HEADS, LANES))
    consts = [mean_rot, tile2(q_g) * (A_HEAD_DIM ** -0.5), tile2(k_g), sink_rows]
    nb = SEQS_PER_STEP
    act_spec = lambda width: pl.BlockSpec((nb, t, width), lambda b, c: (b, c, 0))
    return pl.pallas_call(
        _swa_kernel,
        grid=(bsz // nb, seq // t),
        in_specs=[act_spec(d), act_spec(d), act_spec(2 * A_KV), act_spec(1),
                  pl.BlockSpec((None, t, 4 * t), lambda b, c: (jnp.minimum(c, 1), 0, 0))]
        + [_resident(a, 2) for a in consts],
        out_specs=act_spec(d),
        out_shape=jax.ShapeDtypeStruct((bsz, seq, d), BF16),
        scratch_shapes=[pltpu.VMEM((nb * A_KV_HEADS, 4 * t, LANES), BF16),
                        pltpu.VMEM((nb * A_KV_HEADS, 4 * t, 2 * LANES), BF16)],
        compiler_params=_params("parallel", "arbitrary"),
        name="swa",
    )(q, z, kv, pos, bias, *consts)


def kernel(x, c, positions, ada_w, ada_b, norm_g, ev_w_in, s5_a_re, s5_a_im, s5_log_dt, s5_b_re, s5_b_im, s5_c_re, s5_c_im, s5_d, s5_w_glu, m_conv_w, m_conv_b, m_wq, m_wk, m_wv, m_wo, m_w_if, m_b_if, m_norm_g, m_skip, ev_w_out, od_w_in, od_q_norm_g, od_k_norm_g, od_sinks, od_w_out):
    bsz, seq, d = x.shape
    assert d == D_MODEL and seq % CHUNK == 0 and bsz * S5_PERM_T == 2 * LANES
    mod = _adaln(c, ada_w, ada_b)
    h = x.astype(F32)
    pos = positions.astype(F32)[:, :, None]
    a_q = A_HEADS * A_HEAD_DIM

    def opening(layer):
        i = layer // 2
        if layer % 2 == 0:
            return (layer, norm_g[layer], ev_w_in[i], [d] * 4)
        w = od_w_in[i]
        w = jnp.concatenate([w[:, :a_q], w[:, a_q + 2 * A_KV:], w[:, a_q:a_q + 2 * A_KV]], axis=1)
        return (layer, norm_g[layer], w, [a_q, a_q, 2 * A_KV])

    proj = _boundary(h, mod, open_=opening(0))
    for layer in range(DEPTH):
        i = layer // 2
        if layer % 2 == 0:
            s5_u, s5_z, m_in, m_z = proj
            tables = _s5_tables(s5_a_re[i], s5_a_im[i], s5_log_dt[i], s5_b_re[i], s5_b_im[i],
                                s5_c_re[i], s5_c_im[i])
            y_s5 = _s5(s5_u, s5_z, tables, s5_d[i], s5_w_glu[i])
            y_m = _mlstm(m_in, m_z, m_conv_w[i], m_conv_b[i], m_wq[i], m_wk[i], m_wv[i], m_wo[i],
                         m_w_if[i], m_b_if[i], m_norm_g[i], m_skip[i])
            closing = (layer, [y_s5, y_m], ev_w_out[i])
        else:
            q, z, kv = proj
            attn = _swa(q, z, kv, pos, od_q_norm_g[i], od_k_norm_g[i], od_sinks[i])
            closing = (layer, [attn], od_w_out[i])
        h, *proj = _boundary(h, mod, close=closing, open_=opening(layer + 1) if layer + 1 < DEPTH else None)
    return h.astype(x.dtype)
```

```python
import functools
import math

import jax
import jax.numpy as jnp
from jax import lax
from jax.experimental import pallas as pl
from jax.experimental.pallas import tpu as pltpu

F32 = jnp.float32
BF16 = jnp.bfloat16
HIGHEST = lax.Precision.HIGHEST

EPS = 1e-6
D_MODEL = 1024
DEPTH = 4
S5_GROUP = 16
S5_STATE = 64
S5_GROUPS = D_MODEL // S5_GROUP
S5_GROUPS_PER_TILE = 16
S5_TILES = S5_GROUPS // S5_GROUPS_PER_TILE
S5_TILE_CH = S5_GROUPS_PER_TILE * S5_GROUP
S5_TILE_ST = S5_GROUPS_PER_TILE * S5_STATE
S5_SCAN_COLS = 256
S5_TIME_TILE = 64
S5_PERM_T = 16
M_HEADS = 4
M_HEAD_DIM = D_MODEL // M_HEADS
M_CONV = 4
CHUNK = 128
CONV_TAIL = 16
A_HEAD_DIM = 64
A_HEADS = D_MODEL // A_HEAD_DIM
A_KV_HEADS = A_HEADS // 4
A_KV = A_KV_HEADS * A_HEAD_DIM
ROPE_THETA = 500000.0
ROPE_DIM = A_HEAD_DIM // 4
ROPE_HALF = ROPE_DIM // 2
LANES = 128
NEG_BIG = -1e30

ROW_TILE = 1024
SEQS_PER_STEP = 2
M_SEQS_PER_STEP = 4
VMEM_LIMIT = 56 * 1024 * 1024
NT_DIMS = (((1,), (1,)), ((), ()))


_sigmoid = jax.nn.sigmoid


def _silu(x):
    return x * _sigmoid(x)


def _gelu_tanh(x):
    return 0.5 * x * (1.0 + jnp.tanh(math.sqrt(2.0 / math.pi) * (x + 0.044715 * (x * x * x))))


def _params(*sem):
    return pltpu.CompilerParams(dimension_semantics=sem, vmem_limit_bytes=VMEM_LIMIT)


def _resident(arr, n_grid):
    zeros = (0,) * arr.ndim
    index_map = (lambda i: zeros) if n_grid == 1 else (lambda i, j: zeros)
    return pl.BlockSpec(arr.shape, index_map, pipeline_mode=pl.Buffered(1))


def _adaln_kernel(c_ref, w_ref, b_ref, o_ref):
    s = _silu(c_ref[...])
    o_ref[...] = jnp.dot(s, w_ref[...], preferred_element_type=F32, precision=HIGHEST) + b_ref[...]


def _adaln(c, ada_w, ada_b):
    bsz = c.shape[0]
    mod = pl.pallas_call(
        _adaln_kernel,
        grid=(DEPTH, 3),
        in_specs=[
            pl.BlockSpec((bsz, D_MODEL), lambda l, j: (0, 0)),
            pl.BlockSpec((None, D_MODEL, D_MODEL), lambda l, j: (l, 0, j)),
            pl.BlockSpec((None, None, 1, D_MODEL), lambda l, j: (l, j, 0, 0)),
        ],
        out_specs=pl.BlockSpec((None, None, bsz, D_MODEL), lambda l, j: (l, j, 0, 0)),
        out_shape=jax.ShapeDtypeStruct((DEPTH, 3, bsz, D_MODEL), F32),
        compiler_params=_params("parallel", "parallel"),
        name="adaln",
    )(c.astype(F32), ada_w.astype(F32), ada_b.astype(F32).reshape(DEPTH, 3, 1, D_MODEL))
    return jnp.transpose(mod, (0, 2, 1, 3))


def _mod_spec(layer):
    return pl.BlockSpec((None, None, 3, D_MODEL), lambda b, i: (layer, b, 0, 0))


def _boundary_kernel(*refs, n_parts, n_outs):
    refs = list(refs)
    h_ref = refs.pop(0)
    if n_parts:
        mod_out_ref = refs.pop(0)
        y_refs = [refs.pop(0) for _ in range(n_parts)]
        w_out_ref = refs.pop(0)
    if n_outs:
        mod_in_ref, g_ref, w_in_ref = refs.pop(0), refs.pop(0), refs.pop(0)
    h = h_ref[...]
    if n_parts:
        acc = None
        for i, y_ref in enumerate(y_refs):
            k = y_ref.shape[-1]
            part = jnp.dot(y_ref[...], w_out_ref[i * k:(i + 1) * k, :], preferred_element_type=F32)
            acc = part if acc is None else acc + part
        h = h + mod_out_ref[2:3, :] * acc
        refs.pop(0)[...] = h
    if n_outs:
        ms = jnp.mean(h * h, axis=-1, keepdims=True)
        hn = h * lax.rsqrt(ms + EPS) * g_ref[...] * (1.0 + mod_in_ref[1:2, :]) + mod_in_ref[0:1, :]
        hb = hn.astype(BF16)
        c0 = 0
        for o_ref in refs:
            width = o_ref.shape[-1]
            o_ref[...] = jnp.dot(hb, w_in_ref[:, c0:c0 + width], preferred_element_type=F32).astype(o_ref.dtype)
            c0 += width


def _boundary(h, mod, close=None, open_=None):
    bsz, seq, d = h.shape
    rt = min(ROW_TILE, seq)
    act_spec = lambda width: pl.BlockSpec((None, rt, width), lambda b, i: (b, i, 0))
    args, in_specs, out_specs, out_shape = [h], [act_spec(d)], [], []
    n_parts = n_outs = 0
    if close is not None:
        layer, parts, w_out = close
        n_parts = len(parts)
        w_out = w_out.astype(BF16)
        args += [mod, *parts, w_out]
        in_specs += [_mod_spec(layer)] + [act_spec(p.shape[-1]) for p in parts] + [_resident(w_out, 2)]
        out_specs.append(act_spec(d))
        out_shape.append(jax.ShapeDtypeStruct((bsz, seq, d), F32))
    if open_ is not None:
        layer, gain, w_in, widths = open_
        assert sum(widths) == w_in.shape[1]
        n_outs = len(widths)
        w_in = w_in.astype(BF16)
        args += [mod, gain.astype(F32).reshape(1, d), w_in]
        in_specs += [_mod_spec(layer), pl.BlockSpec((1, d), lambda b, i: (0, 0)), _resident(w_in, 2)]
        out_specs += [act_spec(wd) for wd in widths]
        out_shape += [jax.ShapeDtypeStruct((bsz, seq, wd), BF16) for wd in widths]
    return pl.pallas_call(
        functools.partial(_boundary_kernel, n_parts=n_parts, n_outs=n_outs),
        grid=(bsz, seq // rt),
        in_specs=in_specs,
        out_specs=out_specs,
        out_shape=out_shape,
        compiler_params=_params("parallel", "parallel"),
        name="boundary",
    )(*args)


def _s5_kernel(u_ref, z_ref, perm_ref, permt_ref, bmat_ref, cmat_ref, are_ref, aim_ref, d_ref, wglu_ref,
               o_ref, st_re, st_im, u_tm, z_tm, x_re, x_im, y_acc):
    bsz, tt, d = u_ref.shape
    rows = tt * bsz
    pt = S5_PERM_T
    blk_rows = pt * bsz

    @pl.when(pl.program_id(0) == 0)
    def _():
        st_re[...] = jnp.zeros_like(st_re)
        st_im[...] = jnp.zeros_like(st_im)

    for src_ref, dst in ((u_ref, u_tm), (z_ref, z_tm)):
        for tb in range(tt // pt):
            by_seq = jnp.concatenate([src_ref[b, tb * pt:(tb + 1) * pt, :] for b in range(bsz)], axis=0)
            dst[tb * blk_rows:(tb + 1) * blk_rows, :] = jnp.dot(
                perm_ref[...], by_seq, preferred_element_type=F32).astype(BF16)

    u = u_tm[...]

    def project_in(j):
        u_j = u[:, j * S5_TILE_CH:(j + 1) * S5_TILE_CH]
        x_re[j] = jnp.dot(u_j, bmat_ref[j, :, :S5_TILE_ST], preferred_element_type=F32)
        x_im[j] = jnp.dot(u_j, bmat_ref[j, :, S5_TILE_ST:], preferred_element_type=F32)

    def recur(j):
        for c0 in range(0, S5_TILE_ST, S5_SCAN_COLS):
            cols = slice(c0, c0 + S5_SCAN_COLS)
            st_cols = slice(j * S5_TILE_ST + c0, j * S5_TILE_ST + c0 + S5_SCAN_COLS)
            a_re = jnp.broadcast_to(are_ref[j, :, cols], (bsz, S5_SCAN_COLS))
            a_im = jnp.broadcast_to(aim_ref[j, :, cols], (bsz, S5_SCAN_COLS))
            s_re, s_im = st_re[:, st_cols], st_im[:, st_cols]
            for t in range(tt):
                r = slice(t * bsz, (t + 1) * bsz)
                s_re, s_im = (a_re * s_re - a_im * s_im + x_re[j, r, cols],
                              a_re * s_im + a_im * s_re + x_im[j, r, cols])
                x_re[j, r, cols] = s_re
                x_im[j, r, cols] = s_im
            st_re[:, st_cols] = s_re
            st_im[:, st_cols] = s_im

    def project_out(j):
        ch = slice(j * S5_TILE_CH, (j + 1) * S5_TILE_CH)
        y_j = (jnp.dot(x_re[j].astype(BF16), cmat_ref[j, :S5_TILE_ST, :], preferred_element_type=F32)
               + jnp.dot(x_im[j].astype(BF16), cmat_ref[j, S5_TILE_ST:, :], preferred_element_type=F32))
        y_acc[:, ch] = _gelu_tanh(y_j + d_ref[:, ch] * u[:, ch].astype(F32))

    project_in(0)
    for j in range(S5_TILES):
        if j + 1 < S5_TILES:
            project_in(j + 1)
        recur(j)
        project_out(j)

    def glu_lin(k):
        return jnp.dot(y_acc[k * blk_rows:(k + 1) * blk_rows, :].astype(BF16), wglu_ref[...],
                       preferred_element_type=F32)

    n_blk = tt // pt
    lin = {0: glu_lin(0)}
    for k in range(n_blk):
        if k + 1 < n_blk:
            lin[k + 1] = glu_lin(k + 1)
        r = slice(k * blk_rows, (k + 1) * blk_rows)
        out = (y_acc[r, :] * _sigmoid(lin.pop(k)) * _silu(z_tm[r, :].astype(F32))).astype(BF16)
        by_seq = jnp.dot(permt_ref[...], out, preferred_element_type=F32).astype(o_ref.dtype)
        for b in range(bsz):
            o_ref[b, k * pt:(k + 1) * pt, :] = by_seq[b * pt:(b + 1) * pt, :]


def _s5_tables(a_re, a_im, log_dt, b_re, b_im, c_re, c_im):
    g, n, p = S5_GROUPS, S5_STATE, S5_GROUP
    gt, nt = S5_GROUPS_PER_TILE, S5_TILES
    a = lax.complex(a_re.astype(F32), a_im.astype(F32))
    dt = jnp.exp(log_dt.astype(F32))[:, None]
    a_bar = jnp.exp(a * dt)
    b_bar = ((a_bar - 1.0) / a)[..., None] * lax.complex(b_re.astype(F32), b_im.astype(F32))
    eye = jnp.eye(gt, dtype=F32)

    def in_block(b):
        return jnp.einsum('jgnp,gh->jgphn', b.reshape(nt, gt, n, p), eye).reshape(nt, gt * p, gt * n)

    def out_block(c):
        return jnp.einsum('jgpn,gh->jgnhp', c.reshape(nt, gt, p, n), eye).reshape(nt, gt * n, gt * p)

    bmat = jnp.concatenate([in_block(b_bar.real), in_block(b_bar.imag)], axis=-1)
    cmat = jnp.concatenate([out_block(c_re.astype(F32)), out_block(-c_im.astype(F32))], axis=1)
    a_cols = lambda v: v.reshape(nt, 1, gt * n)
    return bmat.astype(BF16), cmat.astype(BF16), a_cols(a_bar.real), a_cols(a_bar.imag)


def _s5(u, z, tables, d_skip, w_glu):
    bsz, seq, d = u.shape
    tt = min(S5_TIME_TILE, seq)
    rows = tt * bsz
    bmat, cmat, a_re, a_im = tables
    blk_rows = S5_PERM_T * bsz
    perm = jnp.eye(blk_rows, dtype=BF16).reshape(bsz, S5_PERM_T, blk_rows).transpose(1, 0, 2).reshape(blk_rows, blk_rows)
    consts = [perm, perm.T, bmat, cmat, a_re, a_im, d_skip.astype(F32).reshape(1, d), w_glu.astype(BF16)]
    act_spec = pl.BlockSpec((bsz, tt, d), lambda i: (0, i, 0))
    return pl.pallas_call(
        _s5_kernel,
        grid=(seq // tt,),
        in_specs=[act_spec, act_spec] + [_resident(a, 1) for a in consts],
        out_specs=act_spec,
        out_shape=jax.ShapeDtypeStruct((bsz, seq, d), BF16),
        scratch_shapes=[
            pltpu.VMEM((bsz, S5_GROUPS * S5_STATE), F32),
            pltpu.VMEM((bsz, S5_GROUPS * S5_STATE), F32),
            pltpu.VMEM((rows, d), BF16),
            pltpu.VMEM((rows, d), BF16),
            pltpu.VMEM((S5_TILES, rows, S5_TILE_ST), F32),
            pltpu.VMEM((S5_TILES, rows, S5_TILE_ST), F32),
            pltpu.VMEM((rows, d), F32),
        ],
        compiler_params=_params("arbitrary"),
        name="s5",
    )(u, z, *consts)


def _mlstm_kernel(mi_ref, mz_ref, shift_ref, convw_ref, convb_ref, wq_ref, wkt_ref, wvo_ref, wif_ref, bif_ref,
                  ng_ref, skip_ref, o_ref, ext, c_st, n_st, m_st):
    t = CHUNK
    dh = M_HEAD_DIM
    seqs = range(mi_ref.shape[0])
    heads = range(M_HEADS)
    units = [(sq, h) for sq in seqs for h in heads]
    hsl = [slice(h * dh, (h + 1) * dh) for h in heads]
    slot = lambda sq, h: sq * M_HEADS + h

    @pl.when(pl.program_id(1) == 0)
    def _():
        for sq in seqs:
            ext[sq, 0:CONV_TAIL, :] = jnp.zeros((CONV_TAIL, D_MODEL), BF16)
        c_st[...] = jnp.zeros_like(c_st)
        n_st[...] = jnp.zeros_like(n_st)
        m_st[...] = jnp.zeros_like(m_st)

    tri = lax.broadcasted_iota(jnp.int32, (t, t), 0) >= lax.broadcasted_iota(jnp.int32, (t, t), 1)
    tri_f = tri.astype(F32)

    def project(sq, f):
        mi_b = mi_ref[sq]
        ext[sq, CONV_TAIL:, :] = mi_b
        taps = jnp.dot(shift_ref[...], ext[sq], preferred_element_type=F32)
        ext[sq, 0:CONV_TAIL, :] = mi_b[t - CONV_TAIL:, :]
        f["vo"] = [jnp.dot(mi_b[:, hsl[h]], wvo_ref[h], preferred_element_type=F32) for h in heads]
        yield
        conv = convb_ref[...] + mi_b.astype(F32) * convw_ref[M_CONV - 1:M_CONV, :]
        for k in range(M_CONV - 1):
            conv = conv + taps[k * t:(k + 1) * t, :] * convw_ref[k:k + 1, :]
        f["xc"] = _silu(conv)
        xc_b = f["xc"].astype(BF16)
        f["gates"] = (jnp.dot(xc_b, wif_ref[0:D_MODEL, :], preferred_element_type=F32)
                      + jnp.dot(mi_b, wif_ref[D_MODEL:, :], preferred_element_type=F32) + bif_ref[...])
        f["q"] = [jnp.dot(xc_b[:, hsl[h]], wq_ref[h], preferred_element_type=F32) for h in heads]
        f["kt"] = [lax.dot_general(wkt_ref[h], xc_b[:, hsl[h]], NT_DIMS, preferred_element_type=F32)
                   for h in heads]
        yield
        f["cum"] = jnp.dot(tri_f, jax.nn.log_sigmoid(f["gates"]), preferred_element_type=F32, precision=HIGHEST)
        q_b = [f["q"][h].astype(BF16) for h in heads]
        f["kt_b"] = [f["kt"][h].astype(BF16) for h in heads]
        f["c_prev"] = [c_st[slot(sq, h)] for h in heads]
        f["qkt"] = [jnp.dot(q_b[h], f["kt_b"][h], preferred_element_type=F32) for h in heads]
        f["qc"] = [jnp.dot(q_b[h], f["c_prev"][h].astype(BF16), preferred_element_type=F32) for h in heads]
        yield
        f["g_row"], f["b_row"] = f["gates"].T, f["cum"].T

    def cell(sq, f):
        st = [slot(sq, h) for h in heads]
        n_prev = [n_st[st[h]] for h in heads]
        m_prev = [m_st[st[h]] for h in heads]
        v_b = [f["vo"][h][:, :dh].astype(BF16) for h in heads]
        s, g, m_t, w_row, kwt, decay = [], [], [], [], [], []
        for h in heads:
            bc = f["cum"][:, M_HEADS + h:M_HEADS + h + 1]
            br = f["b_row"][M_HEADS + h:M_HEADS + h + 1, :]
            li = f["g_row"][h:h + 1, :]
            d_log = jnp.where(tri, bc - br + li, -jnp.inf)
            inter = bc + m_prev[h]
            m_t.append(jnp.maximum(inter, jnp.max(d_log, axis=-1, keepdims=True)))
            s.append(f["qkt"][h] * jnp.exp(d_log - m_t[h]))
            g.append(jnp.exp(inter - m_t[h]))
            b_last = br[:, t - 1:t]
            w_log = b_last - br + li
            m_new = jnp.maximum(b_last + m_prev[h], jnp.max(w_log, axis=-1, keepdims=True))
            w_row.append(jnp.exp(w_log - m_new))
            kwt.append((f["kt"][h] * w_row[h]).astype(BF16))
            decay.append(jnp.exp(b_last + m_prev[h] - m_new))
            m_st[st[h]] = m_new
            yield
        sv = [jnp.dot(s[h].astype(BF16), v_b[h], preferred_element_type=F32) for h in heads]
        kv = [jnp.dot(kwt[h], v_b[h], preferred_element_type=F32) for h in heads]
        ksum = [lax.dot_general(w_row[h].astype(BF16), f["kt_b"][h], NT_DIMS, preferred_element_type=F32)
                for h in heads]
        yield
        for h in heads:
            c_st[st[h]] = decay[h] * f["c_prev"][h] + kv[h]
            n_st[st[h]] = decay[h] * n_prev[h] + ksum[h]
            num = sv[h] + g[h] * f["qc"][h]
            den = (jnp.sum(s[h], axis=-1, keepdims=True)
                   + g[h] * jnp.sum(f["q"][h] * n_prev[h], axis=-1, keepdims=True))
            inv = 1.0 / jnp.maximum(jnp.abs(den), jnp.exp(-m_t[h]))
            oh = _sigmoid(f["vo"][h][:, dh:]) * (num * inv)
            ms = jnp.mean(oh * oh, axis=-1, keepdims=True)
            y = oh * lax.rsqrt(ms + EPS) * ng_ref[:, hsl[h]] + skip_ref[:, hsl[h]] * f["xc"][:, hsl[h]]
            o_ref[sq, :, hsl[h]] = (y * _silu(mz_ref[sq, :, hsl[h]].astype(F32))).astype(o_ref.dtype)
            yield

    def alternate(*gens):
        live = list(gens)
        while live:
            for gen in list(live):
                if next(gen, "done") == "done":
                    live.remove(gen)

    fronts = [{} for _ in seqs]
    alternate(project(0, fronts[0]))
    for sq in seqs[1:]:
        alternate(project(sq, fronts[sq]), cell(sq - 1, fronts[sq - 1]))
    alternate(cell(seqs[-1], fronts[seqs[-1]]))


def _mlstm(m_in, m_z, conv_w, conv_b, wq, wk, wv, wo, w_if, b_if, norm_g, skip):
    bsz, seq, d = m_in.shape
    t = CHUNK
    wkt = jnp.swapaxes(wk * (M_HEAD_DIM ** -0.5), 1, 2).astype(BF16)
    wvo = jnp.concatenate([wv, wo], axis=-1).astype(BF16)
    pad = LANES - 2 * M_HEADS
    w_if = jnp.pad(w_if.astype(BF16), ((0, 0), (0, pad)))
    b_if = jnp.pad(b_if.astype(F32), (0, pad)).reshape(1, LANES)
    rows = jnp.arange((M_CONV - 1) * t)
    src = CONV_TAIL + rows % t + rows // t - (M_CONV - 1)
    shift = (src[:, None] == jnp.arange(CONV_TAIL + t)[None, :]).astype(BF16)
    consts = [shift, conv_w.astype(F32), conv_b.astype(F32).reshape(1, d), wq.astype(BF16), wkt, wvo, w_if, b_if,
              norm_g.astype(F32).reshape(1, d), skip.astype(F32).reshape(1, d)]
    nb = M_SEQS_PER_STEP
    act_spec = pl.BlockSpec((nb, CHUNK, d), lambda b, c: (b, c, 0))
    return pl.pallas_call(
        _mlstm_kernel,
        grid=(bsz // nb, seq // CHUNK),
        in_specs=[act_spec, act_spec] + [_resident(a, 2) for a in consts],
        out_specs=act_spec,
        out_shape=jax.ShapeDtypeStruct((bsz, seq, d), BF16),
        scratch_shapes=[
            pltpu.VMEM((nb, CONV_TAIL + CHUNK, d), BF16),
            pltpu.VMEM((nb * M_HEADS, M_HEAD_DIM, M_HEAD_DIM), F32),
            pltpu.VMEM((nb * M_HEADS, 1, M_HEAD_DIM), F32),
            pltpu.VMEM((nb * M_HEADS, 1, 1), F32),
        ],
        compiler_params=_params("parallel", "arbitrary"),
        name="mlstm",
    )(m_in, m_z, *consts)


def _rope_tables(pos):
    lane = lax.broadcasted_iota(jnp.int32, (1, LANES), 1)
    in_head = lane % A_HEAD_DIM
    freq = (lane % ROPE_HALF).astype(F32)
    inv_freq = jnp.exp(-math.log(ROPE_THETA) * freq / ROPE_HALF)
    ang = pos * inv_freq
    cos, sin = jnp.cos(ang), jnp.sin(ang)
    first, second = in_head < ROPE_HALF, (in_head >= ROPE_HALF) & (in_head < ROPE_DIM)
    cos_t = jnp.where(first | second, cos, 1.0)
    sin_t = jnp.where(first, -sin, jnp.where(second, sin, 0.0))
    return cos_t, sin_t


def _norm_rope(x, gain, mean_rot, tables):
    cos_t, sin_t = tables
    t = x.shape[0]
    n = x.shape[1] // LANES
    xs = jnp.concatenate([x[:, s * LANES:(s + 1) * LANES] for s in range(n)], axis=0)
    xg = xs * gain
    both = jnp.dot(jnp.concatenate([(xs * xs).astype(BF16), xg.astype(BF16)], axis=1), mean_rot,
                   preferred_element_type=F32)
    ms, partner = both[:, :LANES], both[:, LANES:]
    roped = xg.reshape(n, t, LANES) * cos_t + partner.reshape(n, t, LANES) * sin_t
    return roped.reshape(n * t, LANES) * lax.rsqrt(ms + EPS)


def _swa_kernel(q_ref, z_ref, kv_ref, pos_ref, bias_ref, mrot_ref, qg_ref, kg_ref, sink_ref, o_ref,
                k_buf, v_buf):
    t = CHUNK
    hd = A_HEAD_DIM
    blk = pl.program_id(1)
    n_seq = q_ref.shape[0]
    low = lax.broadcasted_iota(jnp.int32, (1, LANES), 1) < hd

    @pl.when(blk == 0)
    def _():
        ones = jnp.broadcast_to(jnp.where(low, 1.0, 0.0), (2 * t, LANES))
        for kv in range(n_seq * A_KV_HEADS):
            for part in range(2):
                k_buf[kv, 2 * part * t:(2 * part + 1) * t, :] = jnp.zeros((t, LANES), BF16)
                v_buf[kv, 2 * part * t:(2 * part + 1) * t, 0:LANES] = jnp.zeros((t, LANES), BF16)
                v_buf[kv, 2 * part * t:(2 * part + 2) * t, LANES:] = (ones if part == 0 else 1.0 - ones).astype(BF16)

    @pl.when(blk > 0)
    def _():
        for kv in range(n_seq * A_KV_HEADS):
            for part in range(2):
                k_buf[kv, 2 * part * t:(2 * part + 1) * t, :] = k_buf[kv, (2 * part + 1) * t:(2 * part + 2) * t, :]
                v_buf[kv, 2 * part * t:(2 * part + 1) * t, 0:LANES] = (
                    v_buf[kv, (2 * part + 1) * t:(2 * part + 2) * t, 0:LANES])

    mean_rot = mrot_ref[...]
    bias = bias_ref[...]
    upper = lax.broadcasted_iota(jnp.int32, (2 * t, 1), 0) < t
    for sq in range(n_seq):
        tables = _rope_tables(pos_ref[sq])
        base = sq * A_KV_HEADS
        kr = _norm_rope(kv_ref[sq, :, 0:A_KV].astype(F32), kg_ref[...], mean_rot, tables)
        for s in range(A_KV // LANES):
            for src, buf in ((kr[s * t:(s + 1) * t], k_buf),
                             (kv_ref[sq, :, A_KV + s * LANES:A_KV + (s + 1) * LANES].astype(F32), v_buf)):
                lo = jnp.where(low, src, 0.0)
                hi = jnp.where(low, 0.0, src)
                buf[base + 2 * s, t:2 * t, 0:LANES] = lo.astype(BF16)
                buf[base + 2 * s, 3 * t:4 * t, 0:LANES] = pltpu.roll(lo, hd, 1).astype(BF16)
                buf[base + 2 * s + 1, t:2 * t, 0:LANES] = pltpu.roll(hi, hd, 1).astype(BF16)
                buf[base + 2 * s + 1, 3 * t:4 * t, 0:LANES] = hi.astype(BF16)

        qr = _norm_rope(q_ref[sq].astype(F32), qg_ref[...], mean_rot, tables).astype(BF16)
        scores = [lax.dot_general(qr[2 * kv * t:(2 * kv + 2) * t], k_buf[base + kv], NT_DIMS,
                                  preferred_element_type=F32)
                  for kv in range(A_KV_HEADS)]
        outs = []
        for kv in range(A_KV_HEADS):
            sc = (scores[kv].reshape(2, t, 4 * t) + bias).reshape(2 * t, 4 * t)
            probs, sink_terms = [], []
            for half in range(2):
                sink = jnp.where(upper, sink_ref[4 * kv + half:4 * kv + half + 1, :],
                                 sink_ref[4 * kv + 2 + half:4 * kv + 3 + half, :])
                sc_h = [sc[:, (2 * half + c) * t:(2 * half + c + 1) * t] for c in range(2)]
                m = jnp.maximum(jnp.max(jnp.maximum(sc_h[0], sc_h[1]), axis=-1, keepdims=True), sink)
                probs += [jnp.exp(sc_h[c] - m).astype(BF16) for c in range(2)]
                sink_terms.append(jnp.exp(sink - m))
            o = jnp.dot(jnp.concatenate(probs, axis=1), v_buf[base + kv], preferred_element_type=F32)
            outs.append(o[:, :LANES] / (o[:, LANES:] + jnp.where(low, sink_terms[0], sink_terms[1])))
        for kv in range(A_KV_HEADS):
            for pair in range(2):
                qs = slice((2 * kv + pair) * LANES, (2 * kv + pair + 1) * LANES)
                o_ref[sq, :, qs] = (outs[kv][pair * t:(pair + 1) * t]
                                    * _silu(z_ref[sq, :, qs].astype(F32))).astype(o_ref.dtype)


def _swa(q, z, kv, pos, q_g, k_g, sinks):
    bsz, seq, d = q.shape
    t = CHUNK
    tile2 = lambda g: jnp.tile(g.astype(F32).reshape(1, A_HEAD_DIM), (1, LANES // A_HEAD_DIM))
    qi = jnp.arange(t)[:, None]
    ki = jnp.arange(t)[None, :]
    cur = jnp.where(ki <= qi, 0.0, NEG_BIG).astype(F32)
    prev = jnp.where(ki > qi, 0.0, NEG_BIG).astype(F32)
    first = jnp.concatenate([jnp.full((t, t), NEG_BIG, F32), cur], axis=1)
    later = jnp.concatenate([prev, cur], axis=1)
    bias = jnp.stack([jnp.tile(first, (1, 2)), jnp.tile(later, (1, 2))])
    lane = jnp.arange(LANES)
    in_head = lane % A_HEAD_DIM
    same_head = (lane[:, None] // A_HEAD_DIM == lane[None, :] // A_HEAD_DIM)
    partner = jnp.where(in_head < ROPE_HALF, lane + ROPE_HALF, jnp.where(in_head < ROPE_DIM, lane - ROPE_HALF, -1))
    zeros = jnp.zeros((LANES, LANES), F32)
    mean_rot = jnp.block([[same_head.astype(F32) / A_HEAD_DIM, zeros],
                          [zeros, (lane[:, None] == partner[None, :]).astype(F32)]]).astype(BF16)
    sink_rows = jnp.broadcast_to(sinks.astype(F32).reshape(A_HEADS, 1), (A_HEADS, LANES))
    consts = [mean_rot, tile2(q_g) * (A_HEAD_DIM ** -0.5), tile2(k_g), sink_rows]
    nb = SEQS_PER_STEP
    act_spec = lambda width: pl.BlockSpec((nb, t, width), lambda b, c: (b, c, 0))
    return pl.pallas_call(
        _swa_kernel,
        grid=(bsz // nb, seq // t),
        in_specs=[act_spec(d), act_spec(d), act_spec(2 * A_KV), act_spec(1),
                  pl.BlockSpec((None, t, 4 * t), lambda b, c: (jnp.minimum(c, 1), 0, 0))]
        + [_resident(a, 2) for a in consts],
        out_specs=act_spec(d),
        out_shape=jax.ShapeDtypeStruct((bsz, seq, d), BF16),
        scratch_shapes=[pltpu.VMEM((nb * A_KV_HEADS, 4 * t, LANES), BF16),
                        pltpu.VMEM((nb * A_KV_HEADS, 4 * t, 2 * LANES), BF16)],
        compiler_params=_params("parallel", "arbitrary"),
        name="swa",
    )(q, z, kv, pos, bias, *consts)


def kernel(x, c, positions, ada_w, ada_b, norm_g, ev_w_in, s5_a_re, s5_a_im, s5_log_dt, s5_b_re, s5_b_im, s5_c_re, s5_c_im, s5_d, s5_w_glu, m_conv_w, m_conv_b, m_wq, m_wk, m_wv, m_wo, m_w_if, m_b_if, m_norm_g, m_skip, ev_w_out, od_w_in, od_q_norm_g, od_k_norm_g, od_sinks, od_w_out):
    bsz, seq, d = x.shape
    assert d == D_MODEL and seq % CHUNK == 0 and bsz * S5_PERM_T == 2 * LANES
    mod = _adaln(c, ada_w, ada_b)
    h = x.astype(F32)
    pos = positions.astype(F32)[:, :, None]
    a_q = A_HEADS * A_HEAD_DIM

    def opening(layer):
        i = layer // 2
        if layer % 2 == 0:
            return (layer, norm_g[layer], ev_w_in[i], [d] * 4)
        w = od_w_in[i]
        w = jnp.concatenate([w[:, :a_q], w[:, a_q + 2 * A_KV:], w[:, a_q:a_q + 2 * A_KV]], axis=1)
        return (layer, norm_g[layer], w, [a_q, a_q, 2 * A_KV])

    proj = _boundary(h, mod, open_=opening(0))
    for layer in range(DEPTH):
        i = layer // 2
        if layer % 2 == 0:
            s5_u, s5_z, m_in, m_z = proj
            tables = _s5_tables(s5_a_re[i], s5_a_im[i], s5_log_dt[i], s5_b_re[i], s5_b_im[i],
                                s5_c_re[i], s5_c_im[i])
            y_s5 = _s5(s5_u, s5_z, tables, s5_d[i], s5_w_glu[i])
            y_m = _mlstm(m_in, m_z, m_conv_w[i], m_conv_b[i], m_wq[i], m_wk[i], m_wv[i], m_wo[i],
                         m_w_if[i], m_b_if[i], m_norm_g[i], m_skip[i])
            closing = (layer, [y_s5, y_m], ev_w_out[i])
        else:
            q, z, kv = proj
            attn = _swa(q, z, kv, pos, od_q_norm_g[i], od_k_norm_g[i], od_sinks[i])
            closing = (layer, [attn], od_w_out[i])
        h, *proj = _boundary(h, mod, close=closing, open_=opening(layer + 1) if layer + 1 < DEPTH else None)
    return h.astype(x.dtype)
```

```python
import functools
import math

import jax
import jax.numpy as jnp
from jax import lax
from jax.experimental import pallas as pl
from jax.experimental.pallas import tpu as pltpu

F32 = jnp.float32
BF16 = jnp.bfloat16
HIGHEST = lax.Precision.HIGHEST

EPS = 1e-6
D_MODEL = 1024
DEPTH = 4
S5_GROUP = 16
S5_STATE = 64
S5_GROUPS = D_MODEL // S5_GROUP
S5_GROUPS_PER_TILE = 16
S5_TILES = S5_GROUPS // S5_GROUPS_PER_TILE
S5_TILE_CH = S5_GROUPS_PER_TILE * S5_GROUP
S5_TILE_ST = S5_GROUPS_PER_TILE * S5_STATE
S5_SCAN_COLS = 256
S5_TIME_TILE = 64
S5_PERM_T = 16
M_HEADS = 4
M_HEAD_DIM = D_MODEL // M_HEADS
M_CONV = 4
CHUNK = 128
CONV_TAIL = 16
A_HEAD_DIM = 64
A_HEADS = D_MODEL // A_HEAD_DIM
A_KV_HEADS = A_HEADS // 4
A_KV = A_KV_HEADS * A_HEAD_DIM
ROPE_THETA = 500000.0
ROPE_DIM = A_HEAD_DIM // 4
ROPE_HALF = ROPE_DIM // 2
LANES = 128
NEG_BIG = -1e30

ROW_TILE = 1024
SEQS_PER_STEP = 4
M_SEQS_PER_STEP = 4
VMEM_LIMIT = 56 * 1024 * 1024
NT_DIMS = (((1,), (1,)), ((), ()))


_sigmoid = jax.nn.sigmoid


def _silu(x):
    return x * _sigmoid(x)


def _gelu_tanh(x):
    return 0.5 * x * (1.0 + jnp.tanh(math.sqrt(2.0 / math.pi) * (x + 0.044715 * (x * x * x))))


def _params(*sem):
    return pltpu.CompilerParams(dimension_semantics=sem, vmem_limit_bytes=VMEM_LIMIT)


def _resident(arr, n_grid):
    zeros = (0,) * arr.ndim
    index_map = (lambda i: zeros) if n_grid == 1 else (lambda i, j: zeros)
    return pl.BlockSpec(arr.shape, index_map, pipeline_mode=pl.Buffered(1))


def _adaln_kernel(c_ref, w_ref, b_ref, o_ref):
    s = _silu(c_ref[...])
    o_ref[...] = jnp.dot(s, w_ref[...], preferred_element_type=F32, precision=HIGHEST) + b_ref[...]


def _adaln(c, ada_w, ada_b):
    bsz = c.shape[0]
    mod = pl.pallas_call(
        _adaln_kernel,
        grid=(DEPTH, 3),
        in_specs=[
            pl.BlockSpec((bsz, D_MODEL), lambda l, j: (0, 0)),
            pl.BlockSpec((None, D_MODEL, D_MODEL), lambda l, j: (l, 0, j)),
            pl.BlockSpec((None, None, 1, D_MODEL), lambda l, j: (l, j, 0, 0)),
        ],
        out_specs=pl.BlockSpec((None, None, bsz, D_MODEL), lambda l, j: (l, j, 0, 0)),
        out_shape=jax.ShapeDtypeStruct((DEPTH, 3, bsz, D_MODEL), F32),
        compiler_params=_params("parallel", "parallel"),
        name="adaln",
    )(c.astype(F32), ada_w.astype(F32), ada_b.astype(F32).reshape(DEPTH, 3, 1, D_MODEL))
    return jnp.transpose(mod, (0, 2, 1, 3))


def _mod_spec(layer):
    return pl.BlockSpec((None, None, 3, D_MODEL), lambda b, i: (layer, b, 0, 0))


def _boundary_kernel(*refs, n_parts, n_outs):
    refs = list(refs)
    h_ref = refs.pop(0)
    if n_parts:
        mod_out_ref = refs.pop(0)
        y_refs = [refs.pop(0) for _ in range(n_parts)]
        w_out_ref = refs.pop(0)
    if n_outs:
        mod_in_ref, g_ref, w_in_ref = refs.pop(0), refs.pop(0), refs.pop(0)
    h = h_ref[...]
    if n_parts:
        acc = None
        for i, y_ref in enumerate(y_refs):
            k = y_ref.shape[-1]
            part = jnp.dot(y_ref[...], w_out_ref[i * k:(i + 1) * k, :], preferred_element_type=F32)
            acc = part if acc is None else acc + part
        h = h + mod_out_ref[2:3, :] * acc
        refs.pop(0)[...] = h
    if n_outs:
        ms = jnp.mean(h * h, axis=-1, keepdims=True)
        hn = h * lax.rsqrt(ms + EPS) * g_ref[...] * (1.0 + mod_in_ref[1:2, :]) + mod_in_ref[0:1, :]
        hb = hn.astype(BF16)
        c0 = 0
        for o_ref in refs:
            width = o_ref.shape[-1]
            o_ref[...] = jnp.dot(hb, w_in_ref[:, c0:c0 + width], preferred_element_type=F32).astype(o_ref.dtype)
            c0 += width


def _boundary(h, mod, close=None, open_=None):
    bsz, seq, d = h.shape
    rt = min(ROW_TILE, seq)
    act_spec = lambda width: pl.BlockSpec((None, rt, width), lambda b, i: (b, i, 0))
    args, in_specs, out_specs, out_shape = [h], [act_spec(d)], [], []
    n_parts = n_outs = 0
    if close is not None:
        layer, parts, w_out = close
        n_parts = len(parts)
        w_out = w_out.astype(BF16)
        args += [mod, *parts, w_out]
        in_specs += [_mod_spec(layer)] + [act_spec(p.shape[-1]) for p in parts] + [_resident(w_out, 2)]
        out_specs.append(act_spec(d))
        out_shape.append(jax.ShapeDtypeStruct((bsz, seq, d), F32))
    if open_ is not None:
        layer, gain, w_in, widths = open_
        assert sum(widths) == w_in.shape[1]
        n_outs = len(widths)
        w_in = w_in.astype(BF16)
        args += [mod, gain.astype(F32).reshape(1, d), w_in]
        in_specs += [_mod_spec(layer), pl.BlockSpec((1, d), lambda b, i: (0, 0)), _resident(w_in, 2)]
        out_specs += [act_spec(wd) for wd in widths]
        out_shape += [jax.ShapeDtypeStruct((bsz, seq, wd), BF16) for wd in widths]
    return pl.pallas_call(
        functools.partial(_boundary_kernel, n_parts=n_parts, n_outs=n_outs),
        grid=(bsz, seq // rt),
        in_specs=in_specs,
        out_specs=out_specs,
        out_shape=out_shape,
        compiler_params=_params("parallel", "parallel"),
        name="boundary",
    )(*args)


def _s5_kernel(u_ref, z_ref, perm_ref, permt_ref, bmat_ref, cmat_ref, are_ref, aim_ref, d_ref, wglu_ref,
               o_ref, st_re, st_im, u_tm, x_re, x_im, y_acc):
    bsz, tt, d = u_ref.shape
    rows = tt * bsz
    pt = S5_PERM_T
    blk_rows = pt * bsz

    @pl.when(pl.program_id(0) == 0)
    def _():
        st_re[...] = jnp.zeros_like(st_re)
        st_im[...] = jnp.zeros_like(st_im)

    for tb in range(tt // pt):
        by_seq = jnp.concatenate([u_ref[b, tb * pt:(tb + 1) * pt, :] for b in range(bsz)], axis=0)
        u_tm[tb * blk_rows:(tb + 1) * blk_rows, :] = jnp.dot(
            perm_ref[...], by_seq, preferred_element_type=F32).astype(BF16)

    u = u_tm[...]

    def project_in(j):
        u_j = u[:, j * S5_TILE_CH:(j + 1) * S5_TILE_CH]
        x_re[j] = jnp.dot(u_j, bmat_ref[j, :, :S5_TILE_ST], preferred_element_type=F32)
        x_im[j] = jnp.dot(u_j, bmat_ref[j, :, S5_TILE_ST:], preferred_element_type=F32)

    def recur(j):
        for c0 in range(0, S5_TILE_ST, S5_SCAN_COLS):
            cols = slice(c0, c0 + S5_SCAN_COLS)
            st_cols = slice(j * S5_TILE_ST + c0, j * S5_TILE_ST + c0 + S5_SCAN_COLS)
            a_re = jnp.broadcast_to(are_ref[j, :, cols], (bsz, S5_SCAN_COLS))
            a_im = jnp.broadcast_to(aim_ref[j, :, cols], (bsz, S5_SCAN_COLS))
            s_re, s_im = st_re[:, st_cols], st_im[:, st_cols]
            for t in range(tt):
                r = slice(t * bsz, (t + 1) * bsz)
                s_re, s_im = (a_re * s_re - a_im * s_im + x_re[j, r, cols],
                              a_re * s_im + a_im * s_re + x_im[j, r, cols])
                x_re[j, r, cols] = s_re
                x_im[j, r, cols] = s_im
            st_re[:, st_cols] = s_re
            st_im[:, st_cols] = s_im

    def project_out(j):
        ch = slice(j * S5_TILE_CH, (j + 1) * S5_TILE_CH)
        y_j = (jnp.dot(x_re[j].astype(BF16), cmat_ref[j, :S5_TILE_ST, :], preferred_element_type=F32)
               + jnp.dot(x_im[j].astype(BF16), cmat_ref[j, S5_TILE_ST:, :], preferred_element_type=F32))
        y_acc[:, ch] = _gelu_tanh(y_j + d_ref[:, ch] * u[:, ch].astype(F32))

    project_in(0)
    for j in range(S5_TILES):
        if j + 1 < S5_TILES:
            project_in(j + 1)
        recur(j)
        project_out(j)

    def glu_lin(k):
        return jnp.dot(y_acc[k * blk_rows:(k + 1) * blk_rows, :].astype(BF16), wglu_ref[...],
                       preferred_element_type=F32)

    n_blk = tt // pt
    lin = {0: glu_lin(0)}
    for k in range(n_blk):
        if k + 1 < n_blk:
            lin[k + 1] = glu_lin(k + 1)
        r = slice(k * blk_rows, (k + 1) * blk_rows)
        glu = (y_acc[r, :] * _sigmoid(lin.pop(k))).astype(BF16)
        by_seq = jnp.dot(permt_ref[...], glu, preferred_element_type=F32)
        for b in range(bsz):
            gate = _silu(z_ref[b, k * pt:(k + 1) * pt, :].astype(F32))
            o_ref[b, k * pt:(k + 1) * pt, :] = (by_seq[b * pt:(b + 1) * pt, :] * gate).astype(o_ref.dtype)


def _s5_tables(a_re, a_im, log_dt, b_re, b_im, c_re, c_im):
    g, n, p = S5_GROUPS, S5_STATE, S5_GROUP
    gt, nt = S5_GROUPS_PER_TILE, S5_TILES
    a = lax.complex(a_re.astype(F32), a_im.astype(F32))
    dt = jnp.exp(log_dt.astype(F32))[:, None]
    a_bar = jnp.exp(a * dt)
    b_bar = ((a_bar - 1.0) / a)[..., None] * lax.complex(b_re.astype(F32), b_im.astype(F32))
    eye = jnp.eye(gt, dtype=F32)

    def in_block(b):
        return jnp.einsum('jgnp,gh->jgphn', b.reshape(nt, gt, n, p), eye).reshape(nt, gt * p, gt * n)

    def out_block(c):
        return jnp.einsum('jgpn,gh->jgnhp', c.reshape(nt, gt, p, n), eye).reshape(nt, gt * n, gt * p)

    bmat = jnp.concatenate([in_block(b_bar.real), in_block(b_bar.imag)], axis=-1)
    cmat = jnp.concatenate([out_block(c_re.astype(F32)), out_block(-c_im.astype(F32))], axis=1)
    a_cols = lambda v: v.reshape(nt, 1, gt * n)
    return bmat.astype(BF16), cmat.astype(BF16), a_cols(a_bar.real), a_cols(a_bar.imag)


def _s5(u, z, tables, d_skip, w_glu):
    bsz, seq, d = u.shape
    tt = min(S5_TIME_TILE, seq)
    rows = tt * bsz
    bmat, cmat, a_re, a_im = tables
    blk_rows = S5_PERM_T * bsz
    perm = jnp.eye(blk_rows, dtype=BF16).reshape(bsz, S5_PERM_T, blk_rows).transpose(1, 0, 2).reshape(blk_rows, blk_rows)
    consts = [perm, perm.T, bmat, cmat, a_re, a_im, d_skip.astype(F32).reshape(1, d), w_glu.astype(BF16)]
    act_spec = pl.BlockSpec((bsz, tt, d), lambda i: (0, i, 0))
    return pl.pallas_call(
        _s5_kernel,
        grid=(seq // tt,),
        in_specs=[act_spec, act_spec] + [_resident(a, 1) for a in consts],
        out_specs=act_spec,
        out_shape=jax.ShapeDtypeStruct((bsz, seq, d), BF16),
        scratch_shapes=[
            pltpu.VMEM((bsz, S5_GROUPS * S5_STATE), F32),
            pltpu.VMEM((bsz, S5_GROUPS * S5_STATE), F32),
            pltpu.VMEM((rows, d), BF16),
            pltpu.VMEM((S5_TILES, rows, S5_TILE_ST), F32),
            pltpu.VMEM((S5_TILES, rows, S5_TILE_ST), F32),
            pltpu.VMEM((rows, d), F32),
        ],
        compiler_params=_params("arbitrary"),
        name="s5",
    )(u, z, *consts)


def _mlstm_kernel(mi_ref, mz_ref, shift_ref, convw_ref, convb_ref, wq_ref, wkt_ref, wvo_ref, wif_ref, bif_ref,
                  ng_ref, skip_ref, o_ref, ext, c_st, n_st, m_st):
    t = CHUNK
    dh = M_HEAD_DIM
    seqs = range(mi_ref.shape[0])
    heads = range(M_HEADS)
    units = [(sq, h) for sq in seqs for h in heads]
    hsl = [slice(h * dh, (h + 1) * dh) for h in heads]
    slot = lambda sq, h: sq * M_HEADS + h

    @pl.when(pl.program_id(1) == 0)
    def _():
        for sq in seqs:
            ext[sq, 0:CONV_TAIL, :] = jnp.zeros((CONV_TAIL, D_MODEL), BF16)
        c_st[...] = jnp.zeros_like(c_st)
        n_st[...] = jnp.zeros_like(n_st)
        m_st[...] = jnp.zeros_like(m_st)

    tri = lax.broadcasted_iota(jnp.int32, (t, t), 0) >= lax.broadcasted_iota(jnp.int32, (t, t), 1)
    tri_b = tri.astype(BF16)

    def project(sq, f):
        mi_b = mi_ref[sq]
        ext[sq, CONV_TAIL:, :] = mi_b
        taps = jnp.dot(shift_ref[...], ext[sq], preferred_element_type=F32)
        ext[sq, 0:CONV_TAIL, :] = mi_b[t - CONV_TAIL:, :]
        f["vo"] = [jnp.dot(mi_b[:, hsl[h]], wvo_ref[h], preferred_element_type=F32) for h in heads]
        yield
        conv = convb_ref[...] + mi_b.astype(F32) * convw_ref[M_CONV - 1:M_CONV, :]
        for k in range(M_CONV - 1):
            conv = conv + taps[k * t:(k + 1) * t, :] * convw_ref[k:k + 1, :]
        f["xc"] = _silu(conv)
        xc_b = f["xc"].astype(BF16)
        f["gates"] = (jnp.dot(xc_b, wif_ref[0:D_MODEL, :], preferred_element_type=F32)
                      + jnp.dot(mi_b, wif_ref[D_MODEL:, :], preferred_element_type=F32) + bif_ref[...])
        f["q"] = [jnp.dot(xc_b[:, hsl[h]], wq_ref[h], preferred_element_type=F32) for h in heads]
        f["kt"] = [lax.dot_general(wkt_ref[h], xc_b[:, hsl[h]], NT_DIMS, preferred_element_type=F32)
                   for h in heads]
        yield
        rest = jax.nn.log_sigmoid(f["gates"])
        f["cum"] = None
        for _ in range(3):
            term = rest.astype(BF16)
            part = jnp.dot(tri_b, term, preferred_element_type=F32)
            f["cum"] = part if f["cum"] is None else f["cum"] + part
            rest = rest - term.astype(F32)
        q_b = [f["q"][h].astype(BF16) for h in heads]
        f["kt_b"] = [f["kt"][h].astype(BF16) for h in heads]
        f["c_prev"] = [c_st[slot(sq, h)] for h in heads]
        f["qkt"] = [jnp.dot(q_b[h], f["kt_b"][h], preferred_element_type=F32) for h in heads]
        f["qc"] = [jnp.dot(q_b[h], f["c_prev"][h].astype(BF16), preferred_element_type=F32) for h in heads]
        yield
        f["g_row"], f["b_row"] = f["gates"].T, f["cum"].T

    def cell(sq, f):
        st = [slot(sq, h) for h in heads]
        n_prev = [n_st[st[h]] for h in heads]
        m_prev = [m_st[st[h]] for h in heads]
        v_b = [f["vo"][h][:, :dh].astype(BF16) for h in heads]
        s, g, m_t, w_row, kwt, decay = [], [], [], [], [], []
        for h in heads:
            bc = f["cum"][:, M_HEADS + h:M_HEADS + h + 1]
            br = f["b_row"][M_HEADS + h:M_HEADS + h + 1, :]
            li = f["g_row"][h:h + 1, :]
            d_log = jnp.where(tri, bc - br + li, -jnp.inf)
            inter = bc + m_prev[h]
            m_t.append(jnp.maximum(inter, jnp.max(d_log, axis=-1, keepdims=True)))
            s.append(f["qkt"][h] * jnp.exp(d_log - m_t[h]))
            g.append(jnp.exp(inter - m_t[h]))
            b_last = br[:, t - 1:t]
            w_log = b_last - br + li
            m_new = jnp.maximum(b_last + m_prev[h], jnp.max(w_log, axis=-1, keepdims=True))
            w_row.append(jnp.exp(w_log - m_new))
            kwt.append((f["kt"][h] * w_row[h]).astype(BF16))
            decay.append(jnp.exp(b_last + m_prev[h] - m_new))
            m_st[st[h]] = m_new
            yield
        sv = [jnp.dot(s[h].astype(BF16), v_b[h], preferred_element_type=F32) for h in heads]
        kv = [jnp.dot(kwt[h], v_b[h], preferred_element_type=F32) for h in heads]
        ksum = [lax.dot_general(w_row[h].astype(BF16), f["kt_b"][h], NT_DIMS, preferred_element_type=F32)
                for h in heads]
        yield
        for h in heads:
            c_st[st[h]] = decay[h] * f["c_prev"][h] + kv[h]
            n_st[st[h]] = decay[h] * n_prev[h] + ksum[h]
            num = sv[h] + g[h] * f["qc"][h]
            den = (jnp.sum(s[h], axis=-1, keepdims=True)
                   + g[h] * jnp.sum(f["q"][h] * n_prev[h], axis=-1, keepdims=True))
            inv = 1.0 / jnp.maximum(jnp.abs(den), jnp.exp(-m_t[h]))
            oh = _sigmoid(f["vo"][h][:, dh:]) * (num * inv)
            ms = jnp.mean(oh * oh, axis=-1, keepdims=True)
            y = oh * lax.rsqrt(ms + EPS) * ng_ref[:, hsl[h]] + skip_ref[:, hsl[h]] * f["xc"][:, hsl[h]]
            o_ref[sq, :, hsl[h]] = (y * _silu(mz_ref[sq, :, hsl[h]].astype(F32))).astype(o_ref.dtype)
            yield

    def alternate(*gens):
        live = list(gens)
        while live:
            for gen in list(live):
                if next(gen, "done") == "done":
                    live.remove(gen)

    fronts = [{} for _ in seqs]
    alternate(project(0, fronts[0]))
    for sq in seqs[1:]:
        alternate(project(sq, fronts[sq]), cell(sq - 1, fronts[sq - 1]))
    alternate(cell(seqs[-1], fronts[seqs[-1]]))


def _mlstm(m_in, m_z, conv_w, conv_b, wq, wk, wv, wo, w_if, b_if, norm_g, skip):
    bsz, seq, d = m_in.shape
    t = CHUNK
    wkt = jnp.swapaxes(wk * (M_HEAD_DIM ** -0.5), 1, 2).astype(BF16)
    wvo = jnp.concatenate([wv, wo], axis=-1).astype(BF16)
    pad = LANES - 2 * M_HEADS
    w_if = jnp.pad(w_if.astype(BF16), ((0, 0), (0, pad)))
    b_if = jnp.pad(b_if.astype(F32), (0, pad)).reshape(1, LANES)
    rows = jnp.arange((M_CONV - 1) * t)
    src = CONV_TAIL + rows % t + rows // t - (M_CONV - 1)
    shift = (src[:, None] == jnp.arange(CONV_TAIL + t)[None, :]).astype(BF16)
    consts = [shift, conv_w.astype(F32), conv_b.astype(F32).reshape(1, d), wq.astype(BF16), wkt, wvo, w_if, b_if,
              norm_g.astype(F32).reshape(1, d), skip.astype(F32).reshape(1, d)]
    nb = M_SEQS_PER_STEP
    act_spec = pl.BlockSpec((nb, CHUNK, d), lambda b, c: (b, c, 0))
    return pl.pallas_call(
        _mlstm_kernel,
        grid=(bsz // nb, seq // CHUNK),
        in_specs=[act_spec, act_spec] + [_resident(a, 2) for a in consts],
        out_specs=act_spec,
        out_shape=jax.ShapeDtypeStruct((bsz, seq, d), BF16),
        scratch_shapes=[
            pltpu.VMEM((nb, CONV_TAIL + CHUNK, d), BF16),
            pltpu.VMEM((nb * M_HEADS, M_HEAD_DIM, M_HEAD_DIM), F32),
            pltpu.VMEM((nb * M_HEADS, 1, M_HEAD_DIM), F32),
            pltpu.VMEM((nb * M_HEADS, 1, 1), F32),
        ],
        compiler_params=_params("parallel", "arbitrary"),
        name="mlstm",
    )(m_in, m_z, *consts)


def _rope_tables(pos_row):
    t = pos_row.shape[1]
    freq = lax.broadcasted_iota(jnp.int32, (ROPE_HALF, 1), 0).astype(F32)
    inv_freq = jnp.exp(-math.log(ROPE_THETA) * freq / ROPE_HALF)
    ang = inv_freq * pos_row
    packed = jnp.concatenate([jnp.cos(ang), jnp.sin(ang), jnp.zeros((LANES - ROPE_DIM, t), F32)], axis=0)
    cs = packed.T
    roll = lambda shift: pltpu.roll(cs, shift, 1)
    lane = lax.broadcasted_iota(jnp.int32, (1, LANES), 1)
    sel = lambda lo, a, b: jnp.where((lane >= lo) & (lane < lo + ROPE_HALF), a, b)
    hd = A_HEAD_DIM
    cos_t = sel(0, cs, sel(ROPE_HALF, roll(ROPE_HALF), sel(hd, roll(hd), sel(hd + ROPE_HALF, roll(hd + ROPE_HALF), 1.0))))
    sin_t = sel(0, -roll(LANES - ROPE_HALF),
                sel(ROPE_HALF, cs, sel(hd, -roll(hd - ROPE_HALF), sel(hd + ROPE_HALF, roll(hd), 0.0))))
    return cos_t, sin_t


def _norm_rope(x, gain, mean_rot, tables):
    cos_t, sin_t = tables
    t = x.shape[0]
    n = x.shape[1] // LANES
    xs = jnp.concatenate([x[:, s * LANES:(s + 1) * LANES] for s in range(n)], axis=0)
    xg = xs * gain
    both = jnp.dot(jnp.concatenate([(xs * xs).astype(BF16), xg.astype(BF16)], axis=1), mean_rot,
                   preferred_element_type=F32)
    ms, partner = both[:, :LANES], both[:, LANES:]
    roped = xg.reshape(n, t, LANES) * cos_t + partner.reshape(n, t, LANES) * sin_t
    return roped.reshape(n * t, LANES) * lax.rsqrt(ms + EPS)


def _swa_kernel(q_ref, z_ref, kv_ref, pos_ref, bias_ref, mrot_ref, qg_ref, kg_ref, sink_ref, o_ref,
                k_buf, v_buf):
    t = CHUNK
    hd = A_HEAD_DIM
    blk = pl.program_id(1)
    n_seq = q_ref.shape[0]
    low = lax.broadcasted_iota(jnp.int32, (1, LANES), 1) < hd

    @pl.when(blk == 0)
    def _():
        ones = jnp.broadcast_to(jnp.where(low, 1.0, 0.0), (2 * t, LANES))
        for kv in range(n_seq * A_KV_HEADS):
            for part in range(2):
                k_buf[kv, 2 * part * t:(2 * part + 1) * t, :] = jnp.zeros((t, LANES), BF16)
                v_buf[kv, 2 * part * t:(2 * part + 1) * t, 0:LANES] = jnp.zeros((t, LANES), BF16)
                v_buf[kv, 2 * part * t:(2 * part + 2) * t, LANES:] = (ones if part == 0 else 1.0 - ones).astype(BF16)

    @pl.when(blk > 0)
    def _():
        for kv in range(n_seq * A_KV_HEADS):
            for part in range(2):
                k_buf[kv, 2 * part * t:(2 * part + 1) * t, :] = k_buf[kv, (2 * part + 1) * t:(2 * part + 2) * t, :]
                v_buf[kv, 2 * part * t:(2 * part + 1) * t, 0:LANES] = (
                    v_buf[kv, (2 * part + 1) * t:(2 * part + 2) * t, 0:LANES])

    mean_rot = mrot_ref[...]
    bias = bias_ref[...]
    upper = lax.broadcasted_iota(jnp.int32, (2 * t, 1), 0) < t
    for sq in range(n_seq):
        tables = _rope_tables(pos_ref[sq])
        base = sq * A_KV_HEADS
        kr = _norm_rope(kv_ref[sq, :, 0:A_KV].astype(F32), kg_ref[...], mean_rot, tables)
        for s in range(A_KV // LANES):
            for src, buf in ((kr[s * t:(s + 1) * t], k_buf),
                             (kv_ref[sq, :, A_KV + s * LANES:A_KV + (s + 1) * LANES].astype(F32), v_buf)):
                lo = jnp.where(low, src, 0.0)
                hi = jnp.where(low, 0.0, src)
                buf[base + 2 * s, t:2 * t, 0:LANES] = lo.astype(BF16)
                buf[base + 2 * s, 3 * t:4 * t, 0:LANES] = pltpu.roll(lo, hd, 1).astype(BF16)
                buf[base + 2 * s + 1, t:2 * t, 0:LANES] = pltpu.roll(hi, hd, 1).astype(BF16)
                buf[base + 2 * s + 1, 3 * t:4 * t, 0:LANES] = hi.astype(BF16)

        qr = _norm_rope(q_ref[sq].astype(F32), qg_ref[...], mean_rot, tables).astype(BF16)
        scores = [lax.dot_general(qr[2 * kv * t:(2 * kv + 2) * t], k_buf[base + kv], NT_DIMS,
                                  preferred_element_type=F32)
                  for kv in range(A_KV_HEADS)]
        outs = []
        for kv in range(A_KV_HEADS):
            sc = (scores[kv].reshape(2, t, 4 * t) + bias).reshape(2 * t, 4 * t)
            probs, sink_terms = [], []
            for half in range(2):
                sink = jnp.where(upper, sink_ref[4 * kv + half:4 * kv + half + 1, :],
                                 sink_ref[4 * kv + 2 + half:4 * kv + 3 + half, :])
                sc_h = [sc[:, (2 * half + c) * t:(2 * half + c + 1) * t] for c in range(2)]
                m = jnp.maximum(jnp.max(jnp.maximum(sc_h[0], sc_h[1]), axis=-1, keepdims=True), sink)
                probs += [jnp.exp2(sc_h[c] - m).astype(BF16) for c in range(2)]
                sink_terms.append(jnp.exp2(sink - m))
            o = jnp.dot(jnp.concatenate(probs, axis=1), v_buf[base + kv], preferred_element_type=F32)
            outs.append(o[:, :LANES] / (o[:, LANES:] + jnp.where(low, sink_terms[0], sink_terms[1])))
        for kv in range(A_KV_HEADS):
            for pair in range(2):
                qs = slice((2 * kv + pair) * LANES, (2 * kv + pair + 1) * LANES)
                o_ref[sq, :, qs] = (outs[kv][pair * t:(pair + 1) * t]
                                    * _silu(z_ref[sq, :, qs].astype(F32))).astype(o_ref.dtype)


def _swa(q, z, kv, pos, q_g, k_g, sinks):
    bsz, seq, d = q.shape
    t = CHUNK
    tile2 = lambda g: jnp.tile(g.astype(F32).reshape(1, A_HEAD_DIM), (1, LANES // A_HEAD_DIM))
    qi = jnp.arange(t)[:, None]
    ki = jnp.arange(t)[None, :]
    cur = jnp.where(ki <= qi, 0.0, NEG_BIG).astype(F32)
    prev = jnp.where(ki > qi, 0.0, NEG_BIG).astype(F32)
    first = jnp.concatenate([jnp.full((t, t), NEG_BIG, F32), cur], axis=1)
    later = jnp.concatenate([prev, cur], axis=1)
    bias = jnp.stack([jnp.tile(first, (1, 2)), jnp.tile(later, (1, 2))])
    lane = jnp.arange(LANES)
    in_head = lane % A_HEAD_DIM
    same_head = (lane[:, None] // A_HEAD_DIM == lane[None, :] // A_HEAD_DIM)
    partner = jnp.where(in_head < ROPE_HALF, lane + ROPE_HALF, jnp.where(in_head < ROPE_DIM, lane - ROPE_HALF, -1))
    zeros = jnp.zeros((LANES, LANES), F32)
    mean_rot = jnp.block([[same_head.astype(F32) / A_HEAD_DIM, zeros],
                          [zeros, (lane[:, None] == partner[None, :]).astype(F32)]]).astype(BF16)
    log2e = math.log2(math.e)
    sink_rows = jnp.broadcast_to(sinks.astype(F32).reshape(A_HEADS, 1) * log2e, (A_HEADS, LANES))
    consts = [mean_rot, tile2(q_g) * (A_HEAD_DIM ** -0.5 * log2e), tile2(k_g), sink_rows]
    nb = SEQS_PER_STEP
    act_spec = lambda width: pl.BlockSpec((nb, t, width), lambda b, c: (b, c, 0))
    return pl.pallas_call(
        _swa_kernel,
        grid=(bsz // nb, seq // t),
        in_specs=[act_spec(d), act_spec(d), act_spec(2 * A_KV),
                  pl.BlockSpec((nb, None, 1, t), lambda b, c: (b, c, 0, 0)),
                  pl.BlockSpec((None, t, 4 * t), lambda b, c: (jnp.minimum(c, 1), 0, 0))]
        + [_resident(a, 2) for a in consts],
        out_specs=act_spec(d),
        out_shape=jax.ShapeDtypeStruct((bsz, seq, d), BF16),
        scratch_shapes=[pltpu.VMEM((nb * A_KV_HEADS, 4 * t, LANES), BF16),
                        pltpu.VMEM((nb * A_KV_HEADS, 4 * t, 2 * LANES), BF16)],
        compiler_params=_params("parallel", "arbitrary"),
        name="swa",
    )(q, z, kv, pos, bias, *consts)


def kernel(x, c, positions, ada_w, ada_b, norm_g, ev_w_in, s5_a_re, s5_a_im, s5_log_dt, s5_b_re, s5_b_im, s5_c_re, s5_c_im, s5_d, s5_w_glu, m_conv_w, m_conv_b, m_wq, m_wk, m_wv, m_wo, m_w_if, m_b_if, m_norm_g, m_skip, ev_w_out, od_w_in, od_q_norm_g, od_k_norm_g, od_sinks, od_w_out):
    bsz, seq, d = x.shape
    assert d == D_MODEL and seq % CHUNK == 0 and bsz * S5_PERM_T == 2 * LANES
    mod = _adaln(c, ada_w, ada_b)
    h = x.astype(F32)
    pos = positions.astype(F32).reshape(bsz, seq // CHUNK, 1, CHUNK)
    a_q = A_HEADS * A_HEAD_DIM

    def opening(layer):
        i = layer // 2
        if layer % 2 == 0:
            return (layer, norm_g[layer], ev_w_in[i], [d] * 4)
        w = od_w_in[i]
        w = jnp.concatenate([w[:, :a_q], w[:, a_q + 2 * A_KV:], w[:, a_q:a_q + 2 * A_KV]], axis=1)
        return (layer, norm_g[layer], w, [a_q, a_q, 2 * A_KV])

    proj = _boundary(h, mod, open_=opening(0))
    for layer in range(DEPTH):
        i = layer // 2
        if layer % 2 == 0:
            s5_u, s5_z, m_in, m_z = proj
            tables = _s5_tables(s5_a_re[i], s5_a_im[i], s5_log_dt[i], s5_b_re[i], s5_b_im[i],
                                s5_c_re[i], s5_c_im[i])
            y_s5 = _s5(s5_u, s5_z, tables, s5_d[i], s5_w_glu[i])
            y_m = _mlstm(m_in, m_z, m_conv_w[i], m_conv_b[i], m_wq[i], m_wk[i], m_wv[i], m_wo[i],
                         m_w_if[i], m_b_if[i], m_norm_g[i], m_skip[i])
            closing = (layer, [y_s5, y_m], ev_w_out[i])
        else:
            q, z, kv = proj
            attn = _swa(q, z, kv, pos, od_q_norm_g[i], od_k_norm_g[i], od_sinks[i])
            closing = (layer, [attn], od_w_out[i])
        h, *proj = _boundary(h, mod, close=closing, open_=opening(layer + 1) if layer + 1 < DEPTH else None)
    return h.astype(x.dtype)
```

```python
import functools
import math

import jax
import jax.numpy as jnp
from jax import lax
from jax.experimental import pallas as pl
from jax.experimental.pallas import tpu as pltpu

F32 = jnp.float32
BF16 = jnp.bfloat16
HIGHEST = lax.Precision.HIGHEST

EPS = 1e-6
D_MODEL = 1024
DEPTH = 4
S5_GROUP = 16
S5_STATE = 64
S5_GROUPS = D_MODEL // S5_GROUP
S5_GROUPS_PER_TILE = 16
S5_TILES = S5_GROUPS // S5_GROUPS_PER_TILE
S5_TILE_CH = S5_GROUPS_PER_TILE * S5_GROUP
S5_TILE_ST = S5_GROUPS_PER_TILE * S5_STATE
S5_SCAN_COLS = 256
S5_TIME_TILE = 64
S5_PERM_T = 16
M_HEADS = 4
M_HEAD_DIM = D_MODEL // M_HEADS
M_CONV = 4
CHUNK = 128
CONV_TAIL = 16
A_HEAD_DIM = 64
A_HEADS = D_MODEL // A_HEAD_DIM
A_KV_HEADS = A_HEADS // 4
A_KV = A_KV_HEADS * A_HEAD_DIM
ROPE_THETA = 500000.0
ROPE_DIM = A_HEAD_DIM // 4
ROPE_HALF = ROPE_DIM // 2
LANES = 128
NEG_BIG = -1e30

ROW_TILE = 1024
SEQS_PER_STEP = 4
M_SEQS_PER_STEP = 4
VMEM_LIMIT = 56 * 1024 * 1024
NT_DIMS = (((1,), (1,)), ((), ()))


_sigmoid = jax.nn.sigmoid


def _silu(x):
    return x * _sigmoid(x)


def _gelu_tanh(x):
    return 0.5 * x * (1.0 + jnp.tanh(math.sqrt(2.0 / math.pi) * (x + 0.044715 * (x * x * x))))


def _params(*sem):
    return pltpu.CompilerParams(dimension_semantics=sem, vmem_limit_bytes=VMEM_LIMIT)


def _resident(arr, n_grid):
    zeros = (0,) * arr.ndim
    index_map = (lambda i: zeros) if n_grid == 1 else (lambda i, j: zeros)
    return pl.BlockSpec(arr.shape, index_map, pipeline_mode=pl.Buffered(1))


def _adaln_kernel(c_ref, w_ref, b_ref, o_ref):
    s = _silu(c_ref[...])
    o_ref[...] = jnp.dot(s, w_ref[...], preferred_element_type=F32, precision=HIGHEST) + b_ref[...]


def _adaln(c, ada_w, ada_b):
    bsz = c.shape[0]
    mod = pl.pallas_call(
        _adaln_kernel,
        grid=(DEPTH, 3),
        in_specs=[
            pl.BlockSpec((bsz, D_MODEL), lambda l, j: (0, 0)),
            pl.BlockSpec((None, D_MODEL, D_MODEL), lambda l, j: (l, 0, j)),
            pl.BlockSpec((None, None, 1, D_MODEL), lambda l, j: (l, j, 0, 0)),
        ],
        out_specs=pl.BlockSpec((None, None, bsz, D_MODEL), lambda l, j: (l, j, 0, 0)),
        out_shape=jax.ShapeDtypeStruct((DEPTH, 3, bsz, D_MODEL), F32),
        compiler_params=_params("parallel", "parallel"),
        name="adaln",
    )(c.astype(F32), ada_w.astype(F32), ada_b.astype(F32).reshape(DEPTH, 3, 1, D_MODEL))
    return jnp.transpose(mod, (0, 2, 1, 3))


def _mod_spec(layer):
    return pl.BlockSpec((None, None, 3, D_MODEL), lambda b, i: (layer, b, 0, 0))


def _boundary_kernel(*refs, n_parts, n_outs):
    refs = list(refs)
    h_ref = refs.pop(0)
    if n_parts:
        mod_out_ref = refs.pop(0)
        y_refs = [refs.pop(0) for _ in range(n_parts)]
        w_out_ref = refs.pop(0)
    if n_outs:
        mod_in_ref, g_ref, w_in_ref = refs.pop(0), refs.pop(0), refs.pop(0)
    h = h_ref[...]
    if n_parts:
        acc = None
        for i, y_ref in enumerate(y_refs):
            k = y_ref.shape[-1]
            part = jnp.dot(y_ref[...], w_out_ref[i * k:(i + 1) * k, :], preferred_element_type=F32)
            acc = part if acc is None else acc + part
        h = h + mod_out_ref[2:3, :] * acc
        refs.pop(0)[...] = h
    if n_outs:
        ms = jnp.mean(h * h, axis=-1, keepdims=True)
        hn = h * lax.rsqrt(ms + EPS) * g_ref[...] * (1.0 + mod_in_ref[1:2, :]) + mod_in_ref[0:1, :]
        hb = hn.astype(BF16)
        c0 = 0
        for o_ref in refs:
            width = o_ref.shape[-1]
            o_ref[...] = jnp.dot(hb, w_in_ref[:, c0:c0 + width], preferred_element_type=F32).astype(o_ref.dtype)
            c0 += width


def _boundary(h, mod, close=None, open_=None):
    bsz, seq, d = h.shape
    rt = min(ROW_TILE, seq)
    act_spec = lambda width: pl.BlockSpec((None, rt, width), lambda b, i: (b, i, 0))
    args, in_specs, out_specs, out_shape = [h], [act_spec(d)], [], []
    n_parts = n_outs = 0
    if close is not None:
        layer, parts, w_out = close
        n_parts = len(parts)
        w_out = w_out.astype(BF16)
        args += [mod, *parts, w_out]
        in_specs += [_mod_spec(layer)] + [act_spec(p.shape[-1]) for p in parts] + [_resident(w_out, 2)]
        out_specs.append(act_spec(d))
        out_shape.append(jax.ShapeDtypeStruct((bsz, seq, d), F32))
    if open_ is not None:
        layer, gain, w_in, widths = open_
        assert sum(widths) == w_in.shape[1]
        n_outs = len(widths)
        w_in = w_in.astype(BF16)
        args += [mod, gain.astype(F32).reshape(1, d), w_in]
        in_specs += [_mod_spec(layer), pl.BlockSpec((1, d), lambda b, i: (0, 0)), _resident(w_in, 2)]
        out_specs += [act_spec(wd) for wd in widths]
        out_shape += [jax.ShapeDtypeStruct((bsz, seq, wd), BF16) for wd in widths]
    return pl.pallas_call(
        functools.partial(_boundary_kernel, n_parts=n_parts, n_outs=n_outs),
        grid=(bsz, seq // rt),
        in_specs=in_specs,
        out_specs=out_specs,
        out_shape=out_shape,
        compiler_params=_params("parallel", "parallel"),
        name="boundary",
    )(*args)


def _s5_kernel(u_ref, z_ref, perm_ref, permt_ref, bmat_ref, cmat_ref, are_ref, aim_ref, d_ref, wglu_ref,
               o_ref, st_re, st_im, u_tm, bu_re, bu_im, x_re, x_im, y_acc):
    bsz, tt, d = u_ref.shape
    rows = tt * bsz
    pt = S5_PERM_T
    blk_rows = pt * bsz

    @pl.when(pl.program_id(0) == 0)
    def _():
        st_re[...] = jnp.zeros_like(st_re)
        st_im[...] = jnp.zeros_like(st_im)

    for tb in range(tt // pt):
        by_seq = jnp.concatenate([u_ref[b, tb * pt:(tb + 1) * pt, :] for b in range(bsz)], axis=0)
        u_tm[tb * blk_rows:(tb + 1) * blk_rows, :] = jnp.dot(
            perm_ref[...], by_seq, preferred_element_type=F32).astype(BF16)

    u = u_tm[...]

    def project_in(j):
        u_j = u[:, j * S5_TILE_CH:(j + 1) * S5_TILE_CH]
        bu_re[j % 2] = jnp.dot(u_j, bmat_ref[j, :, :S5_TILE_ST], preferred_element_type=F32)
        bu_im[j % 2] = jnp.dot(u_j, bmat_ref[j, :, S5_TILE_ST:], preferred_element_type=F32)

    def recur(j):
        for c0 in range(0, S5_TILE_ST, S5_SCAN_COLS):
            cols = slice(c0, c0 + S5_SCAN_COLS)
            st_cols = slice(j * S5_TILE_ST + c0, j * S5_TILE_ST + c0 + S5_SCAN_COLS)
            a_re = jnp.broadcast_to(are_ref[j, :, cols], (bsz, S5_SCAN_COLS))
            a_im = jnp.broadcast_to(aim_ref[j, :, cols], (bsz, S5_SCAN_COLS))
            s_re, s_im = st_re[:, st_cols], st_im[:, st_cols]
            for t in range(tt):
                r = slice(t * bsz, (t + 1) * bsz)
                s_re, s_im = (a_re * s_re - a_im * s_im + bu_re[j % 2, r, cols],
                              a_re * s_im + a_im * s_re + bu_im[j % 2, r, cols])
                x_re[j % 2, r, cols] = s_re.astype(BF16)
                x_im[j % 2, r, cols] = s_im.astype(BF16)
            st_re[:, st_cols] = s_re
            st_im[:, st_cols] = s_im

    def project_out(j):
        ch = slice(j * S5_TILE_CH, (j + 1) * S5_TILE_CH)
        y_j = (jnp.dot(x_re[j % 2], cmat_ref[j, :S5_TILE_ST, :], preferred_element_type=F32)
               + jnp.dot(x_im[j % 2], cmat_ref[j, S5_TILE_ST:, :], preferred_element_type=F32))
        y_acc[:, ch] = _gelu_tanh(y_j + d_ref[:, ch] * u[:, ch].astype(F32))

    project_in(0)
    for j in range(S5_TILES):
        if j + 1 < S5_TILES:
            project_in(j + 1)
        recur(j)
        project_out(j)

    def glu_lin(k):
        return jnp.dot(y_acc[k * blk_rows:(k + 1) * blk_rows, :].astype(BF16), wglu_ref[...],
                       preferred_element_type=F32)

    n_blk = tt // pt
    lin = {0: glu_lin(0)}
    for k in range(n_blk):
        if k + 1 < n_blk:
            lin[k + 1] = glu_lin(k + 1)
        r = slice(k * blk_rows, (k + 1) * blk_rows)
        glu = (y_acc[r, :] * _sigmoid(lin.pop(k))).astype(BF16)
        by_seq = jnp.dot(permt_ref[...], glu, preferred_element_type=F32)
        for b in range(bsz):
            gate = _silu(z_ref[b, k * pt:(k + 1) * pt, :].astype(F32))
            o_ref[b, k * pt:(k + 1) * pt, :] = (by_seq[b * pt:(b + 1) * pt, :] * gate).astype(o_ref.dtype)


def _s5_tables(a_re, a_im, log_dt, b_re, b_im, c_re, c_im):
    g, n, p = S5_GROUPS, S5_STATE, S5_GROUP
    gt, nt = S5_GROUPS_PER_TILE, S5_TILES
    a = lax.complex(a_re.astype(F32), a_im.astype(F32))
    dt = jnp.exp(log_dt.astype(F32))[:, None]
    a_bar = jnp.exp(a * dt)
    b_bar = ((a_bar - 1.0) / a)[..., None] * lax.complex(b_re.astype(F32), b_im.astype(F32))
    eye = jnp.eye(gt, dtype=F32)

    def in_block(b):
        return jnp.einsum('jgnp,gh->jgphn', b.reshape(nt, gt, n, p), eye).reshape(nt, gt * p, gt * n)

    def out_block(c):
        return jnp.einsum('jgpn,gh->jgnhp', c.reshape(nt, gt, p, n), eye).reshape(nt, gt * n, gt * p)

    bmat = jnp.concatenate([in_block(b_bar.real), in_block(b_bar.imag)], axis=-1)
    cmat = jnp.concatenate([out_block(c_re.astype(F32)), out_block(-c_im.astype(F32))], axis=1)
    a_cols = lambda v: v.reshape(nt, 1, gt * n)
    return bmat.astype(BF16), cmat.astype(BF16), a_cols(a_bar.real), a_cols(a_bar.imag)


def _s5(u, z, tables, d_skip, w_glu):
    bsz, seq, d = u.shape
    tt = min(S5_TIME_TILE, seq)
    rows = tt * bsz
    bmat, cmat, a_re, a_im = tables
    blk_rows = S5_PERM_T * bsz
    perm = jnp.eye(blk_rows, dtype=BF16).reshape(bsz, S5_PERM_T, blk_rows).transpose(1, 0, 2).reshape(blk_rows, blk_rows)
    consts = [perm, perm.T, bmat, cmat, a_re, a_im, d_skip.astype(F32).reshape(1, d), w_glu.astype(BF16)]
    act_spec = pl.BlockSpec((bsz, tt, d), lambda i: (0, i, 0))
    return pl.pallas_call(
        _s5_kernel,
        grid=(seq // tt,),
        in_specs=[act_spec, act_spec] + [_resident(a, 1) for a in consts],
        out_specs=act_spec,
        out_shape=jax.ShapeDtypeStruct((bsz, seq, d), BF16),
        scratch_shapes=[
            pltpu.VMEM((bsz, S5_GROUPS * S5_STATE), F32),
            pltpu.VMEM((bsz, S5_GROUPS * S5_STATE), F32),
            pltpu.VMEM((rows, d), BF16),
            pltpu.VMEM((2, rows, S5_TILE_ST), F32),
            pltpu.VMEM((2, rows, S5_TILE_ST), F32),
            pltpu.VMEM((2, rows, S5_TILE_ST), BF16),
            pltpu.VMEM((2, rows, S5_TILE_ST), BF16),
            pltpu.VMEM((rows, d), F32),
        ],
        compiler_params=_params("arbitrary"),
        name="s5",
    )(u, z, *consts)


def _mlstm_kernel(mi_ref, mz_ref, shift_ref, convw_ref, convb_ref, wq_ref, wkt_ref, wvo_ref, wif_ref, bif_ref,
                  ng_ref, skip_ref, o_ref, ext, c_st, n_st, m_st):
    t = CHUNK
    dh = M_HEAD_DIM
    seqs = range(mi_ref.shape[0])
    heads = range(M_HEADS)
    units = [(sq, h) for sq in seqs for h in heads]
    hsl = [slice(h * dh, (h + 1) * dh) for h in heads]
    slot = lambda sq, h: sq * M_HEADS + h

    @pl.when(pl.program_id(1) == 0)
    def _():
        for sq in seqs:
            ext[sq, 0:CONV_TAIL, :] = jnp.zeros((CONV_TAIL, D_MODEL), BF16)
        c_st[...] = jnp.zeros_like(c_st)
        n_st[...] = jnp.zeros_like(n_st)
        m_st[...] = jnp.zeros_like(m_st)

    tri = lax.broadcasted_iota(jnp.int32, (t, t), 0) >= lax.broadcasted_iota(jnp.int32, (t, t), 1)
    tri_b = tri.astype(BF16)

    def project(sq, f):
        mi_b = mi_ref[sq]
        ext[sq, CONV_TAIL:, :] = mi_b
        taps = jnp.dot(shift_ref[...], ext[sq], preferred_element_type=F32)
        ext[sq, 0:CONV_TAIL, :] = mi_b[t - CONV_TAIL:, :]
        f["vo"] = [jnp.dot(mi_b[:, hsl[h]], wvo_ref[h], preferred_element_type=F32) for h in heads]
        yield
        conv = convb_ref[...] + mi_b.astype(F32) * convw_ref[M_CONV - 1:M_CONV, :]
        for k in range(M_CONV - 1):
            conv = conv + taps[k * t:(k + 1) * t, :] * convw_ref[k:k + 1, :]
        f["xc"] = _silu(conv)
        xc_b = f["xc"].astype(BF16)
        f["gates"] = (jnp.dot(xc_b, wif_ref[0:D_MODEL, :], preferred_element_type=F32)
                      + jnp.dot(mi_b, wif_ref[D_MODEL:, :], preferred_element_type=F32) + bif_ref[...])
        f["q"] = [jnp.dot(xc_b[:, hsl[h]], wq_ref[h], preferred_element_type=F32) for h in heads]
        f["kt"] = [lax.dot_general(wkt_ref[h], xc_b[:, hsl[h]], NT_DIMS, preferred_element_type=F32)
                   for h in heads]
        yield
        rest = jax.nn.log_sigmoid(f["gates"])
        f["cum"] = None
        for _ in range(3):
            term = rest.astype(BF16)
            part = jnp.dot(tri_b, term, preferred_element_type=F32)
            f["cum"] = part if f["cum"] is None else f["cum"] + part
            rest = rest - term.astype(F32)
        q_b = [f["q"][h].astype(BF16) for h in heads]
        f["kt_b"] = [f["kt"][h].astype(BF16) for h in heads]
        f["c_prev"] = [c_st[slot(sq, h)] for h in heads]
        f["qkt"] = [jnp.dot(q_b[h], f["kt_b"][h], preferred_element_type=F32) for h in heads]
        f["qc"] = [jnp.dot(q_b[h], f["c_prev"][h].astype(BF16), preferred_element_type=F32) for h in heads]
        yield
        f["g_row"], f["b_row"] = f["gates"].T, f["cum"].T

    def cell(sq, f):
        st = [slot(sq, h) for h in heads]
        n_prev = [n_st[st[h]] for h in heads]
        m_prev = [m_st[st[h]] for h in heads]
        v_b = [f["vo"][h][:, :dh].astype(BF16) for h in heads]
        s, g, m_t, w_row, kwt, decay = [], [], [], [], [], []
        for h in heads:
            bc = f["cum"][:, M_HEADS + h:M_HEADS + h + 1]
            br = f["b_row"][M_HEADS + h:M_HEADS + h + 1, :]
            li = f["g_row"][h:h + 1, :]
            d_log = jnp.where(tri, bc - br + li, -jnp.inf)
            inter = bc + m_prev[h]
            m_t.append(jnp.maximum(inter, jnp.max(d_log, axis=-1, keepdims=True)))
            s.append(f["qkt"][h] * jnp.exp(d_log - m_t[h]))
            g.append(jnp.exp(inter - m_t[h]))
            b_last = br[:, t - 1:t]
            w_log = b_last - br + li
            m_new = jnp.maximum(b_last + m_prev[h], jnp.max(w_log, axis=-1, keepdims=True))
            w_row.append(jnp.exp(w_log - m_new))
            kwt.append((f["kt"][h] * w_row[h]).astype(BF16))
            decay.append(jnp.exp(b_last + m_prev[h] - m_new))
            m_st[st[h]] = m_new
            yield
        sv = [jnp.dot(s[h].astype(BF16), v_b[h], preferred_element_type=F32) for h in heads]
        kv = [jnp.dot(kwt[h], v_b[h], preferred_element_type=F32) for h in heads]
        ksum = [lax.dot_general(w_row[h].astype(BF16), f["kt_b"][h], NT_DIMS, preferred_element_type=F32)
                for h in heads]
        yield
        for h in heads:
            c_st[st[h]] = decay[h] * f["c_prev"][h] + kv[h]
            n_st[st[h]] = decay[h] * n_prev[h] + ksum[h]
            num = sv[h] + g[h] * f["qc"][h]
            den = (jnp.sum(s[h], axis=-1, keepdims=True)
                   + g[h] * jnp.sum(f["q"][h] * n_prev[h], axis=-1, keepdims=True))
            inv = 1.0 / jnp.maximum(jnp.abs(den), jnp.exp(-m_t[h]))
            oh = _sigmoid(f["vo"][h][:, dh:]) * (num * inv)
            ms = jnp.mean(oh * oh, axis=-1, keepdims=True)
            y = oh * lax.rsqrt(ms + EPS) * ng_ref[:, hsl[h]] + skip_ref[:, hsl[h]] * f["xc"][:, hsl[h]]
            o_ref[sq, :, hsl[h]] = (y * _silu(mz_ref[sq, :, hsl[h]].astype(F32))).astype(o_ref.dtype)
            yield

    def alternate(*gens):
        live = list(gens)
        while live:
            for gen in list(live):
                if next(gen, "done") == "done":
                    live.remove(gen)

    fronts = [{} for _ in seqs]
    alternate(project(0, fronts[0]))
    for sq in seqs[1:]:
        alternate(project(sq, fronts[sq]), cell(sq - 1, fronts[sq - 1]))
    alternate(cell(seqs[-1], fronts[seqs[-1]]))


def _mlstm(m_in, m_z, conv_w, conv_b, wq, wk, wv, wo, w_if, b_if, norm_g, skip):
    bsz, seq, d = m_in.shape
    t = CHUNK
    wkt = jnp.swapaxes(wk * (M_HEAD_DIM ** -0.5), 1, 2).astype(BF16)
    wvo = jnp.concatenate([wv, wo], axis=-1).astype(BF16)
    pad = LANES - 2 * M_HEADS
    w_if = jnp.pad(w_if.astype(BF16), ((0, 0), (0, pad)))
    b_if = jnp.pad(b_if.astype(F32), (0, pad)).reshape(1, LANES)
    rows = jnp.arange((M_CONV - 1) * t)
    src = CONV_TAIL + rows % t + rows // t - (M_CONV - 1)
    shift = (src[:, None] == jnp.arange(CONV_TAIL + t)[None, :]).astype(BF16)
    consts = [shift, conv_w.astype(F32), conv_b.astype(F32).reshape(1, d), wq.astype(BF16), wkt, wvo, w_if, b_if,
              norm_g.astype(F32).reshape(1, d), skip.astype(F32).reshape(1, d)]
    nb = M_SEQS_PER_STEP
    act_spec = pl.BlockSpec((nb, CHUNK, d), lambda b, c: (b, c, 0))
    return pl.pallas_call(
        _mlstm_kernel,
        grid=(bsz // nb, seq // CHUNK),
        in_specs=[act_spec, act_spec] + [_resident(a, 2) for a in consts],
        out_specs=act_spec,
        out_shape=jax.ShapeDtypeStruct((bsz, seq, d), BF16),
        scratch_shapes=[
            pltpu.VMEM((nb, CONV_TAIL + CHUNK, d), BF16),
            pltpu.VMEM((nb * M_HEADS, M_HEAD_DIM, M_HEAD_DIM), F32),
            pltpu.VMEM((nb * M_HEADS, 1, M_HEAD_DIM), F32),
            pltpu.VMEM((nb * M_HEADS, 1, 1), F32),
        ],
        compiler_params=_params("parallel", "arbitrary"),
        name="mlstm",
    )(m_in, m_z, *consts)


def _rope_tables(pos_row):
    t = pos_row.shape[1]
    freq = lax.broadcasted_iota(jnp.int32, (ROPE_HALF, 1), 0).astype(F32)
    inv_freq = jnp.exp(-math.log(ROPE_THETA) * freq / ROPE_HALF)
    ang = inv_freq * pos_row
    packed = jnp.concatenate([jnp.cos(ang), jnp.sin(ang), jnp.zeros((LANES - ROPE_DIM, t), F32)], axis=0)
    cs = packed.T
    roll = lambda shift: pltpu.roll(cs, shift, 1)
    lane = lax.broadcasted_iota(jnp.int32, (1, LANES), 1)
    sel = lambda lo, a, b: jnp.where((lane >= lo) & (lane < lo + ROPE_HALF), a, b)
    hd = A_HEAD_DIM
    cos_t = sel(0, cs, sel(ROPE_HALF, roll(ROPE_HALF), sel(hd, roll(hd), sel(hd + ROPE_HALF, roll(hd + ROPE_HALF), 1.0))))
    sin_t = sel(0, -roll(LANES - ROPE_HALF),
                sel(ROPE_HALF, cs, sel(hd, -roll(hd - ROPE_HALF), sel(hd + ROPE_HALF, roll(hd), 0.0))))
    return cos_t, sin_t


def _norm_rope(x, gain, mean_rot, tables):
    cos_t, sin_t = tables
    t = x.shape[0]
    n = x.shape[1] // LANES
    xs = jnp.concatenate([x[:, s * LANES:(s + 1) * LANES] for s in range(n)], axis=0)
    xg = xs * gain
    both = jnp.dot(jnp.concatenate([(xs * xs).astype(BF16), xg.astype(BF16)], axis=1), mean_rot,
                   preferred_element_type=F32)
    ms, partner = both[:, :LANES], both[:, LANES:]
    roped = xg.reshape(n, t, LANES) * cos_t + partner.reshape(n, t, LANES) * sin_t
    return roped.reshape(n * t, LANES) * lax.rsqrt(ms + EPS)


def _swa_kernel(q_ref, z_ref, kv_ref, pos_ref, bias_ref, mrot_ref, qg_ref, kg_ref, sink_ref, o_ref,
                k_buf, v_buf):
    t = CHUNK
    hd = A_HEAD_DIM
    blk = pl.program_id(1)
    n_seq = q_ref.shape[0]
    low = lax.broadcasted_iota(jnp.int32, (1, LANES), 1) < hd

    @pl.when(blk == 0)
    def _():
        ones = jnp.broadcast_to(jnp.where(low, 1.0, 0.0), (2 * t, LANES))
        for kv in range(n_seq * A_KV_HEADS):
            for part in range(2):
                k_buf[kv, 2 * part * t:(2 * part + 1) * t, :] = jnp.zeros((t, LANES), BF16)
                v_buf[kv, 2 * part * t:(2 * part + 1) * t, 0:LANES] = jnp.zeros((t, LANES), BF16)
                v_buf[kv, 2 * part * t:(2 * part + 2) * t, LANES:] = (ones if part == 0 else 1.0 - ones).astype(BF16)

    @pl.when(blk > 0)
    def _():
        for kv in range(n_seq * A_KV_HEADS):
            for part in range(2):
                k_buf[kv, 2 * part * t:(2 * part + 1) * t, :] = k_buf[kv, (2 * part + 1) * t:(2 * part + 2) * t, :]
                v_buf[kv, 2 * part * t:(2 * part + 1) * t, 0:LANES] = (
                    v_buf[kv, (2 * part + 1) * t:(2 * part + 2) * t, 0:LANES])

    mean_rot = mrot_ref[...]
    bias = bias_ref[...]
    upper = lax.broadcasted_iota(jnp.int32, (2 * t, 1), 0) < t
    for sq in range(n_seq):
        tables = _rope_tables(pos_ref[sq])
        base = sq * A_KV_HEADS
        kr = _norm_rope(kv_ref[sq, :, 0:A_KV].astype(F32), kg_ref[...], mean_rot, tables)
        for s in range(A_KV // LANES):
            for src, buf in ((kr[s * t:(s + 1) * t], k_buf),
                             (kv_ref[sq, :, A_KV + s * LANES:A_KV + (s + 1) * LANES].astype(F32), v_buf)):
                lo = jnp.where(low, src, 0.0)
                hi = jnp.where(low, 0.0, src)
                buf[base + 2 * s, t:2 * t, 0:LANES] = lo.astype(BF16)
                buf[base + 2 * s, 3 * t:4 * t, 0:LANES] = pltpu.roll(lo, hd, 1).astype(BF16)
                buf[base + 2 * s + 1, t:2 * t, 0:LANES] = pltpu.roll(hi, hd, 1).astype(BF16)
                buf[base + 2 * s + 1, 3 * t:4 * t, 0:LANES] = hi.astype(BF16)

        qr = _norm_rope(q_ref[sq].astype(F32), qg_ref[...], mean_rot, tables).astype(BF16)
        scores = [lax.dot_general(qr[2 * kv * t:(2 * kv + 2) * t], k_buf[base + kv], NT_DIMS,
                                  preferred_element_type=F32)
                  for kv in range(A_KV_HEADS)]
        outs = []
        for kv in range(A_KV_HEADS):
            sc = (scores[kv].reshape(2, t, 4 * t) + bias).reshape(2 * t, 4 * t)
            probs, sink_terms = [], []
            for half in range(2):
                sink = jnp.where(upper, sink_ref[4 * kv + half:4 * kv + half + 1, :],
                                 sink_ref[4 * kv + 2 + half:4 * kv + 3 + half, :])
                sc_h = [sc[:, (2 * half + c) * t:(2 * half + c + 1) * t] for c in range(2)]
                m = jnp.maximum(jnp.max(jnp.maximum(sc_h[0], sc_h[1]), axis=-1, keepdims=True), sink)
                probs += [jnp.exp2(sc_h[c] - m).astype(BF16) for c in range(2)]
                sink_terms.append(jnp.exp2(sink - m))
            o = jnp.dot(jnp.concatenate(probs, axis=1), v_buf[base + kv], preferred_element_type=F32)
            outs.append(o[:, :LANES] / (o[:, LANES:] + jnp.where(low, sink_terms[0], sink_terms[1])))
        for kv in range(A_KV_HEADS):
            for pair in range(2):
                qs = slice((2 * kv + pair) * LANES, (2 * kv + pair + 1) * LANES)
                o_ref[sq, :, qs] = (outs[kv][pair * t:(pair + 1) * t]
                                    * _silu(z_ref[sq, :, qs].astype(F32))).astype(o_ref.dtype)


def _swa(q, z, kv, pos, q_g, k_g, sinks):
    bsz, seq, d = q.shape
    t = CHUNK
    tile2 = lambda g: jnp.tile(g.astype(F32).reshape(1, A_HEAD_DIM), (1, LANES // A_HEAD_DIM))
    qi = jnp.arange(t)[:, None]
    ki = jnp.arange(t)[None, :]
    cur = jnp.where(ki <= qi, 0.0, NEG_BIG).astype(F32)
    prev = jnp.where(ki > qi, 0.0, NEG_BIG).astype(F32)
    first = jnp.concatenate([jnp.full((t, t), NEG_BIG, F32), cur], axis=1)
    later = jnp.concatenate([prev, cur], axis=1)
    bias = jnp.stack([jnp.tile(first, (1, 2)), jnp.tile(later, (1, 2))])
    lane = jnp.arange(LANES)
    in_head = lane % A_HEAD_DIM
    same_head = (lane[:, None] // A_HEAD_DIM == lane[None, :] // A_HEAD_DIM)
    partner = jnp.where(in_head < ROPE_HALF, lane + ROPE_HALF, jnp.where(in_head < ROPE_DIM, lane - ROPE_HALF, -1))
    zeros = jnp.zeros((LANES, LANES), F32)
    mean_rot = jnp.block([[same_head.astype(F32) / A_HEAD_DIM, zeros],
                          [zeros, (lane[:, None] == partner[None, :]).astype(F32)]]).astype(BF16)
    log2e = math.log2(math.e)
    sink_rows = jnp.broadcast_to(sinks.astype(F32).reshape(A_HEADS, 1) * log2e, (A_HEADS, LANES))
    consts = [mean_rot, tile2(q_g) * (A_HEAD_DIM ** -0.5 * log2e), tile2(k_g), sink_rows]
    nb = SEQS_PER_STEP
    act_spec = lambda width: pl.BlockSpec((nb, t, width), lambda b, c: (b, c, 0))
    return pl.pallas_call(
        _swa_kernel,
        grid=(bsz // nb, seq // t),
        in_specs=[act_spec(d), act_spec(d), act_spec(2 * A_KV),
                  pl.BlockSpec((nb, None, 1, t), lambda b, c: (b, c, 0, 0)),
                  pl.BlockSpec((None, t, 4 * t), lambda b, c: (jnp.minimum(c, 1), 0, 0))]
        + [_resident(a, 2) for a in consts],
        out_specs=act_spec(d),
        out_shape=jax.ShapeDtypeStruct((bsz, seq, d), BF16),
        scratch_shapes=[pltpu.VMEM((nb * A_KV_HEADS, 4 * t, LANES), BF16),
                        pltpu.VMEM((nb * A_KV_HEADS, 4 * t, 2 * LANES), BF16)],
        compiler_params=_params("parallel", "arbitrary"),
        name="swa",
    )(q, z, kv, pos, bias, *consts)


def kernel(x, c, positions, ada_w, ada_b, norm_g, ev_w_in, s5_a_re, s5_a_im, s5_log_dt, s5_b_re, s5_b_im, s5_c_re, s5_c_im, s5_d, s5_w_glu, m_conv_w, m_conv_b, m_wq, m_wk, m_wv, m_wo, m_w_if, m_b_if, m_norm_g, m_skip, ev_w_out, od_w_in, od_q_norm_g, od_k_norm_g, od_sinks, od_w_out):
    bsz, seq, d = x.shape
    assert d == D_MODEL and seq % CHUNK == 0 and bsz * S5_PERM_T == 2 * LANES
    mod = _adaln(c, ada_w, ada_b)
    h = x.astype(F32)
    pos = positions.astype(F32).reshape(bsz, seq // CHUNK, 1, CHUNK)
    a_q = A_HEADS * A_HEAD_DIM

    def opening(layer):
        i = layer // 2
        if layer % 2 == 0:
            return (layer, norm_g[layer], ev_w_in[i], [d] * 4)
        w = od_w_in[i]
        w = jnp.concatenate([w[:, :a_q], w[:, a_q + 2 * A_KV:], w[:, a_q:a_q + 2 * A_KV]], axis=1)
        return (layer, norm_g[layer], w, [a_q, a_q, 2 * A_KV])

    proj = _boundary(h, mod, open_=opening(0))
    for layer in range(DEPTH):
        i = layer // 2
        if layer % 2 == 0:
            s5_u, s5_z, m_in, m_z = proj
            tables = _s5_tables(s5_a_re[i], s5_a_im[i], s5_log_dt[i], s5_b_re[i], s5_b_im[i],
                                s5_c_re[i], s5_c_im[i])
            y_s5 = _s5(s5_u, s5_z, tables, s5_d[i], s5_w_glu[i])
            y_m = _mlstm(m_in, m_z, m_conv_w[i], m_conv_b[i], m_wq[i], m_wk[i], m_wv[i], m_wo[i],
                         m_w_if[i], m_b_if[i], m_norm_g[i], m_skip[i])
            closing = (layer, [y_s5, y_m], ev_w_out[i])
        else:
            q, z, kv = proj
            attn = _swa(q, z, kv, pos, od_q_norm_g[i], od_k_norm_g[i], od_sinks[i])
            closing = (layer, [attn], od_w_out[i])
        h, *proj = _boundary(h, mod, close=closing, open_=opening(layer + 1) if layer + 1 < DEPTH else None)
    return h.astype(x.dtype)
```

```python
import functools
import math

import jax
import jax.numpy as jnp
from jax import lax
from jax.experimental import pallas as pl
from jax.experimental.pallas import tpu as pltpu

F32 = jnp.float32
BF16 = jnp.bfloat16
HIGHEST = lax.Precision.HIGHEST

EPS = 1e-6
D_MODEL = 1024
DEPTH = 4
S5_GROUP = 16
S5_STATE = 64
S5_GROUPS = D_MODEL // S5_GROUP
S5_GROUPS_PER_TILE = 16
S5_TILES = S5_GROUPS // S5_GROUPS_PER_TILE
S5_TILE_CH = S5_GROUPS_PER_TILE * S5_GROUP
S5_TILE_ST = S5_GROUPS_PER_TILE * S5_STATE
S5_SCAN_COLS = 256
S5_TIME_TILE = 64
S5_PERM_T = 16
M_HEADS = 4
M_HEAD_DIM = D_MODEL // M_HEADS
M_CONV = 4
CHUNK = 128
CONV_TAIL = 16
A_HEAD_DIM = 64
A_HEADS = D_MODEL // A_HEAD_DIM
A_KV_HEADS = A_HEADS // 4
A_KV = A_KV_HEADS * A_HEAD_DIM
ROPE_THETA = 500000.0
ROPE_DIM = A_HEAD_DIM // 4
ROPE_HALF = ROPE_DIM // 2
LANES = 128
NEG_BIG = -1e30

ROW_TILE = 1024
SEQS_PER_STEP = 4
M_SEQS_PER_STEP = 4
V7X_VMEM_BYTES = 64 * 1024 * 1024
VMEM_LIMIT = V7X_VMEM_BYTES - 8 * 1024 * 1024
NT_DIMS = (((1,), (1,)), ((), ()))


_sigmoid = jax.nn.sigmoid


def _silu(x):
    return x * _sigmoid(x)


def _gelu_tanh(x):
    return 0.5 * x * (1.0 + jnp.tanh(math.sqrt(2.0 / math.pi) * (x + 0.044715 * (x * x * x))))


def _params(*sem):
    return pltpu.CompilerParams(dimension_semantics=sem, vmem_limit_bytes=VMEM_LIMIT)


def _resident(arr, n_grid):
    zeros = (0,) * arr.ndim
    index_map = (lambda i: zeros) if n_grid == 1 else (lambda i, j: zeros)
    return pl.BlockSpec(arr.shape, index_map, pipeline_mode=pl.Buffered(1))


def _adaln_kernel(c_ref, w_ref, b_ref, o_ref):
    s = _silu(c_ref[...])
    o_ref[...] = jnp.dot(s, w_ref[...], preferred_element_type=F32, precision=HIGHEST) + b_ref[...]


def _adaln(c, ada_w, ada_b):
    bsz = c.shape[0]
    mod = pl.pallas_call(
        _adaln_kernel,
        grid=(DEPTH, 3),
        in_specs=[
            pl.BlockSpec((bsz, D_MODEL), lambda l, j: (0, 0)),
            pl.BlockSpec((None, D_MODEL, D_MODEL), lambda l, j: (l, 0, j)),
            pl.BlockSpec((None, None, 1, D_MODEL), lambda l, j: (l, j, 0, 0)),
        ],
        out_specs=pl.BlockSpec((None, None, bsz, D_MODEL), lambda l, j: (l, j, 0, 0)),
        out_shape=jax.ShapeDtypeStruct((DEPTH, 3, bsz, D_MODEL), F32),
        compiler_params=_params("parallel", "parallel"),
        name="adaln",
    )(c.astype(F32), ada_w.astype(F32), ada_b.astype(F32).reshape(DEPTH, 3, 1, D_MODEL))
    return jnp.transpose(mod, (0, 2, 1, 3))


def _mod_spec(layer):
    return pl.BlockSpec((None, None, 3, D_MODEL), lambda b, i: (layer, b, 0, 0))


def _boundary_kernel(*refs, n_parts, n_outs):
    refs = list(refs)
    h_ref = refs.pop(0)
    if n_parts:
        mod_out_ref = refs.pop(0)
        y_refs = [refs.pop(0) for _ in range(n_parts)]
        w_out_ref = refs.pop(0)
    if n_outs:
        mod_in_ref, g_ref, w_in_ref = refs.pop(0), refs.pop(0), refs.pop(0)
    h = h_ref[...]
    if n_parts:
        acc = None
        for i, y_ref in enumerate(y_refs):
            k = y_ref.shape[-1]
            part = jnp.dot(y_ref[...], w_out_ref[i * k:(i + 1) * k, :], preferred_element_type=F32)
            acc = part if acc is None else acc + part
        h = h + mod_out_ref[2:3, :] * acc
        refs.pop(0)[...] = h
    if n_outs:
        ms = jnp.mean(h * h, axis=-1, keepdims=True)
        hn = h * lax.rsqrt(ms + EPS) * g_ref[...] * (1.0 + mod_in_ref[1:2, :]) + mod_in_ref[0:1, :]
        hb = hn.astype(BF16)
        c0 = 0
        for o_ref in refs:
            width = o_ref.shape[-1]
            o_ref[...] = jnp.dot(hb, w_in_ref[:, c0:c0 + width], preferred_element_type=F32).astype(o_ref.dtype)
            c0 += width


def _boundary(h, mod, close=None, open_=None):
    bsz, seq, d = h.shape
    rt = min(ROW_TILE, seq)
    act_spec = lambda width: pl.BlockSpec((None, rt, width), lambda b, i: (b, i, 0))
    args, in_specs, out_specs, out_shape = [h], [act_spec(d)], [], []
    n_parts = n_outs = 0
    if close is not None:
        layer, parts, w_out = close
        n_parts = len(parts)
        w_out = w_out.astype(BF16)
        args += [mod, *parts, w_out]
        in_specs += [_mod_spec(layer)] + [act_spec(p.shape[-1]) for p in parts] + [_resident(w_out, 2)]
        out_specs.append(act_spec(d))
        out_shape.append(jax.ShapeDtypeStruct((bsz, seq, d), F32))
    if open_ is not None:
        layer, gain, w_in, widths = open_
        assert sum(widths) == w_in.shape[1]
        n_outs = len(widths)
        w_in = w_in.astype(BF16)
        args += [mod, gain.astype(F32).reshape(1, d), w_in]
        in_specs += [_mod_spec(layer), pl.BlockSpec((1, d), lambda b, i: (0, 0)), _resident(w_in, 2)]
        out_specs += [act_spec(wd) for wd in widths]
        out_shape += [jax.ShapeDtypeStruct((bsz, seq, wd), BF16) for wd in widths]
    return pl.pallas_call(
        functools.partial(_boundary_kernel, n_parts=n_parts, n_outs=n_outs),
        grid=(bsz, seq // rt),
        in_specs=in_specs,
        out_specs=out_specs,
        out_shape=out_shape,
        compiler_params=_params("parallel", "parallel"),
        name="boundary",
    )(*args)


def _s5_kernel(u_ref, z_ref, perm_ref, permt_ref, bmat_ref, cmat_ref, are_ref, aim_ref, d_ref, wglu_ref,
               o_ref, st_re, st_im, u_tm, bu_re, bu_im, x_re, x_im, y_acc):
    bsz, tt, d = u_ref.shape
    rows = tt * bsz
    pt = S5_PERM_T
    blk_rows = pt * bsz

    @pl.when(pl.program_id(0) == 0)
    def _():
        st_re[...] = jnp.zeros_like(st_re)
        st_im[...] = jnp.zeros_like(st_im)

    for tb in range(tt // pt):
        by_seq = jnp.concatenate([u_ref[b, tb * pt:(tb + 1) * pt, :] for b in range(bsz)], axis=0)
        u_tm[tb * blk_rows:(tb + 1) * blk_rows, :] = jnp.dot(
            perm_ref[...], by_seq, preferred_element_type=F32).astype(BF16)


    def project_in(j):
        u_j = u_tm[:, j * S5_TILE_CH:(j + 1) * S5_TILE_CH]
        bu_re[j % 2] = jnp.dot(u_j, bmat_ref[j, :, :S5_TILE_ST], preferred_element_type=F32)
        bu_im[j % 2] = jnp.dot(u_j, bmat_ref[j, :, S5_TILE_ST:], preferred_element_type=F32)

    def recur(j):
        for c0 in range(0, S5_TILE_ST, S5_SCAN_COLS):
            cols = slice(c0, c0 + S5_SCAN_COLS)
            st_cols = slice(j * S5_TILE_ST + c0, j * S5_TILE_ST + c0 + S5_SCAN_COLS)
            a_re = jnp.broadcast_to(are_ref[j, :, cols], (bsz, S5_SCAN_COLS))
            a_im = jnp.broadcast_to(aim_ref[j, :, cols], (bsz, S5_SCAN_COLS))
            s_re, s_im = st_re[:, st_cols], st_im[:, st_cols]
            for t in range(tt):
                r = slice(t * bsz, (t + 1) * bsz)
                s_re, s_im = (a_re * s_re - a_im * s_im + bu_re[j % 2, r, cols],
                              a_re * s_im + a_im * s_re + bu_im[j % 2, r, cols])
                x_re[j % 2, r, cols] = s_re.astype(BF16)
                x_im[j % 2, r, cols] = s_im.astype(BF16)
            st_re[:, st_cols] = s_re
            st_im[:, st_cols] = s_im

    def project_out(j):
        ch = slice(j * S5_TILE_CH, (j + 1) * S5_TILE_CH)
        y_j = (jnp.dot(x_re[j % 2], cmat_ref[j, :S5_TILE_ST, :], preferred_element_type=F32)
               + jnp.dot(x_im[j % 2], cmat_ref[j, S5_TILE_ST:, :], preferred_element_type=F32))
        y_acc[:, ch] = _gelu_tanh(y_j + d_ref[:, ch] * u_tm[:, ch].astype(F32))

    project_in(0)
    for j in range(S5_TILES):
        if j + 1 < S5_TILES:
            project_in(j + 1)
        recur(j)
        project_out(j)

    def glu_lin(k):
        return jnp.dot(y_acc[k * blk_rows:(k + 1) * blk_rows, :].astype(BF16), wglu_ref[...],
                       preferred_element_type=F32)

    n_blk = tt // pt
    lin = {0: glu_lin(0)}
    for k in range(n_blk):
        if k + 1 < n_blk:
            lin[k + 1] = glu_lin(k + 1)
        r = slice(k * blk_rows, (k + 1) * blk_rows)
        glu = (y_acc[r, :] * _sigmoid(lin.pop(k))).astype(BF16)
        by_seq = jnp.dot(permt_ref[...], glu, preferred_element_type=F32)
        for b in range(bsz):
            gate = _silu(z_ref[b, k * pt:(k + 1) * pt, :].astype(F32))
            o_ref[b, k * pt:(k + 1) * pt, :] = (by_seq[b * pt:(b + 1) * pt, :] * gate).astype(o_ref.dtype)


def _s5_discretise_kernel(are_ref, aim_ref, logdt_ref, bre_ref, bim_ref, expand_ref,
                          abr_ref, abi_ref, bbr_ref, bbi_ref):
    a_re, a_im = are_ref[...], aim_ref[...]
    dt = jnp.exp(logdt_ref[...])
    mag = jnp.exp(a_re * dt)
    ab_re, ab_im = mag * jnp.cos(a_im * dt), mag * jnp.sin(a_im * dt)
    abr_ref[...] = ab_re
    abi_ref[...] = ab_im
    n_re, n_im = ab_re - 1.0, ab_im
    inv = 1.0 / (a_re * a_re + a_im * a_im)
    f_re = (n_re * a_re + n_im * a_im) * inv
    f_im = (n_im * a_re - n_re * a_im) * inv
    spread = lambda v: jnp.dot(v, expand_ref[...], preferred_element_type=F32, precision=HIGHEST)
    f_re, f_im = spread(f_re), spread(f_im)
    b_re, b_im = bre_ref[...], bim_ref[...]
    bbr_ref[...] = f_re * b_re - f_im * b_im
    bbi_ref[...] = f_re * b_im + f_im * b_re


def _s5_tables(a_re, a_im, log_dt, b_re, b_im, c_re, c_im):
    g, n, p = S5_GROUPS, S5_STATE, S5_GROUP
    gt, nt = S5_GROUPS_PER_TILE, S5_TILES
    expand = jnp.repeat(jnp.eye(n, dtype=F32), p, axis=1)
    ab_re, ab_im, bb_re, bb_im = pl.pallas_call(
        _s5_discretise_kernel,
        out_shape=[jax.ShapeDtypeStruct((g, n), F32)] * 2 + [jax.ShapeDtypeStruct((g, n * p), F32)] * 2,
        name="s5_discretise",
    )(a_re.astype(F32), a_im.astype(F32), log_dt.astype(F32).reshape(g, 1),
      b_re.astype(F32).reshape(g, n * p), b_im.astype(F32).reshape(g, n * p), expand)
    eye = jnp.eye(gt, dtype=F32)

    def in_block(b):
        return jnp.einsum('jgnp,gh->jgphn', b.reshape(nt, gt, n, p), eye).reshape(nt, gt * p, gt * n)

    def out_block(c):
        return jnp.einsum('jgpn,gh->jgnhp', c.reshape(nt, gt, p, n), eye).reshape(nt, gt * n, gt * p)

    bmat = jnp.concatenate([in_block(bb_re), in_block(bb_im)], axis=-1)
    cmat = jnp.concatenate([out_block(c_re.astype(F32)), out_block(-c_im.astype(F32))], axis=1)
    a_cols = lambda v: v.reshape(nt, 1, gt * n)
    return bmat.astype(BF16), cmat.astype(BF16), a_cols(ab_re), a_cols(ab_im)


def _s5(u, z, tables, d_skip, w_glu):
    bsz, seq, d = u.shape
    tt = min(S5_TIME_TILE, seq)
    rows = tt * bsz
    bmat, cmat, a_re, a_im = tables
    blk_rows = S5_PERM_T * bsz
    perm = jnp.eye(blk_rows, dtype=BF16).reshape(bsz, S5_PERM_T, blk_rows).transpose(1, 0, 2).reshape(blk_rows, blk_rows)
    consts = [perm, perm.T, bmat, cmat, a_re, a_im, d_skip.astype(F32).reshape(1, d), w_glu.astype(BF16)]
    act_spec = pl.BlockSpec((bsz, tt, d), lambda i: (0, i, 0))
    return pl.pallas_call(
        _s5_kernel,
        grid=(seq // tt,),
        in_specs=[act_spec, act_spec] + [_resident(a, 1) for a in consts],
        out_specs=act_spec,
        out_shape=jax.ShapeDtypeStruct((bsz, seq, d), BF16),
        scratch_shapes=[
            pltpu.VMEM((bsz, S5_GROUPS * S5_STATE), F32),
            pltpu.VMEM((bsz, S5_GROUPS * S5_STATE), F32),
            pltpu.VMEM((rows, d), BF16),
            pltpu.VMEM((2, rows, S5_TILE_ST), F32),
            pltpu.VMEM((2, rows, S5_TILE_ST), F32),
            pltpu.VMEM((2, rows, S5_TILE_ST), BF16),
            pltpu.VMEM((2, rows, S5_TILE_ST), BF16),
            pltpu.VMEM((rows, d), F32),
        ],
        compiler_params=_params("arbitrary"),
        name="s5",
    )(u, z, *consts)


def _mlstm_kernel(mi_ref, mz_ref, shift_ref, convw_ref, convb_ref, wq_ref, wkt_ref, wvo_ref, wif_ref, bif_ref,
                  ng_ref, skip_ref, o_ref, ext, c_st, n_st, m_st):
    t = CHUNK
    dh = M_HEAD_DIM
    seqs = range(mi_ref.shape[0])
    heads = range(M_HEADS)
    hsl = [slice(h * dh, (h + 1) * dh) for h in heads]
    slot = lambda sq, h: sq * M_HEADS + h

    @pl.when(pl.program_id(1) == 0)
    def _():
        for sq in seqs:
            ext[sq, 0:CONV_TAIL, :] = jnp.zeros((CONV_TAIL, D_MODEL), BF16)
        c_st[...] = jnp.zeros_like(c_st)
        n_st[...] = jnp.zeros_like(n_st)
        m_st[...] = jnp.zeros_like(m_st)

    tri = lax.broadcasted_iota(jnp.int32, (t, t), 0) >= lax.broadcasted_iota(jnp.int32, (t, t), 1)
    tri_b = tri.astype(BF16)

    def project(sq):
        f = {}
        mi_b = mi_ref[sq]
        ext[sq, CONV_TAIL:, :] = mi_b
        taps = jnp.dot(shift_ref[...], ext[sq], preferred_element_type=F32)
        ext[sq, 0:CONV_TAIL, :] = mi_b[t - CONV_TAIL:, :]
        f["vo"] = [jnp.dot(mi_b[:, hsl[h]], wvo_ref[h], preferred_element_type=F32) for h in heads]
        conv = convb_ref[...] + mi_b.astype(F32) * convw_ref[M_CONV - 1:M_CONV, :]
        for k in range(M_CONV - 1):
            conv = conv + taps[k * t:(k + 1) * t, :] * convw_ref[k:k + 1, :]
        f["xc"] = _silu(conv)
        xc_b = f["xc"].astype(BF16)
        f["gates"] = (jnp.dot(xc_b, wif_ref[0:D_MODEL, :], preferred_element_type=F32)
                      + jnp.dot(mi_b, wif_ref[D_MODEL:, :], preferred_element_type=F32) + bif_ref[...])
        f["q"] = [jnp.dot(xc_b[:, hsl[h]], wq_ref[h], preferred_element_type=F32) for h in heads]
        f["kt"] = [lax.dot_general(wkt_ref[h], xc_b[:, hsl[h]], NT_DIMS, preferred_element_type=F32)
                   for h in heads]
        rest = jax.nn.log_sigmoid(f["gates"])
        f["cum"] = None
        for _ in range(3):
            term = rest.astype(BF16)
            part = jnp.dot(tri_b, term, preferred_element_type=F32)
            f["cum"] = part if f["cum"] is None else f["cum"] + part
            rest = rest - term.astype(F32)
        q_b = [f["q"][h].astype(BF16) for h in heads]
        f["kt_b"] = [f["kt"][h].astype(BF16) for h in heads]
        f["c_prev"] = [c_st[slot(sq, h)] for h in heads]
        f["qkt"] = [jnp.dot(q_b[h], f["kt_b"][h], preferred_element_type=F32) for h in heads]
        f["qc"] = [jnp.dot(q_b[h], f["c_prev"][h].astype(BF16), preferred_element_type=F32) for h in heads]
        f["g_row"], f["b_row"] = f["gates"].T, f["cum"].T
        return f

    def cell(sq, f):
        st = [slot(sq, h) for h in heads]
        n_prev = [n_st[st[h]] for h in heads]
        m_prev = [m_st[st[h]] for h in heads]
        v_b = [f["vo"][h][:, :dh].astype(BF16) for h in heads]
        s, g, m_t, w_row, kwt, decay = [], [], [], [], [], []
        for h in heads:
            bc = f["cum"][:, M_HEADS + h:M_HEADS + h + 1]
            br = f["b_row"][M_HEADS + h:M_HEADS + h + 1, :]
            li = f["g_row"][h:h + 1, :]
            d_log = jnp.where(tri, bc - br + li, -jnp.inf)
            inter = bc + m_prev[h]
            m_t.append(jnp.maximum(inter, jnp.max(d_log, axis=-1, keepdims=True)))
            s.append(f["qkt"][h] * jnp.exp(d_log - m_t[h]))
            g.append(jnp.exp(inter - m_t[h]))
            b_last = br[:, t - 1:t]
            w_log = b_last - br + li
            m_new = jnp.maximum(b_last + m_prev[h], jnp.max(w_log, axis=-1, keepdims=True))
            w_row.append(jnp.exp(w_log - m_new))
            kwt.append((f["kt"][h] * w_row[h]).astype(BF16))
            decay.append(jnp.exp(b_last + m_prev[h] - m_new))
            m_st[st[h]] = m_new
        sv = [jnp.dot(s[h].astype(BF16), v_b[h], preferred_element_type=F32) for h in heads]
        kv = [jnp.dot(kwt[h], v_b[h], preferred_element_type=F32) for h in heads]
        ksum = [lax.dot_general(w_row[h].astype(BF16), f["kt_b"][h], NT_DIMS, preferred_element_type=F32)
                for h in heads]
        for h in heads:
            c_st[st[h]] = decay[h] * f["c_prev"][h] + kv[h]
            n_st[st[h]] = decay[h] * n_prev[h] + ksum[h]
            num = sv[h] + g[h] * f["qc"][h]
            den = (jnp.sum(s[h], axis=-1, keepdims=True)
                   + g[h] * jnp.sum(f["q"][h] * n_prev[h], axis=-1, keepdims=True))
            inv = 1.0 / jnp.maximum(jnp.abs(den), jnp.exp(-m_t[h]))
            oh = _sigmoid(f["vo"][h][:, dh:]) * (num * inv)
            ms = jnp.mean(oh * oh, axis=-1, keepdims=True)
            y = oh * lax.rsqrt(ms + EPS) * ng_ref[:, hsl[h]] + skip_ref[:, hsl[h]] * f["xc"][:, hsl[h]]
            o_ref[sq, :, hsl[h]] = (y * _silu(mz_ref[sq, :, hsl[h]].astype(F32))).astype(o_ref.dtype)

    fronts = [project(sq) for sq in seqs]
    for sq in seqs:
        cell(sq, fronts[sq])


def _mlstm(m_in, m_z, conv_w, conv_b, wq, wk, wv, wo, w_if, b_if, norm_g, skip):
    bsz, seq, d = m_in.shape
    t = CHUNK
    wkt = jnp.swapaxes(wk * (M_HEAD_DIM ** -0.5), 1, 2).astype(BF16)
    wvo = jnp.concatenate([wv, wo], axis=-1).astype(BF16)
    pad = LANES - 2 * M_HEADS
    w_if = jnp.pad(w_if.astype(BF16), ((0, 0), (0, pad)))
    b_if = jnp.pad(b_if.astype(F32), (0, pad)).reshape(1, LANES)
    rows = jnp.arange((M_CONV - 1) * t)
    src = CONV_TAIL + rows % t + rows // t - (M_CONV - 1)
    shift = (src[:, None] == jnp.arange(CONV_TAIL + t)[None, :]).astype(BF16)
    consts = [shift, conv_w.astype(F32), conv_b.astype(F32).reshape(1, d), wq.astype(BF16), wkt, wvo, w_if, b_if,
              norm_g.astype(F32).reshape(1, d), skip.astype(F32).reshape(1, d)]
    nb = M_SEQS_PER_STEP
    act_spec = pl.BlockSpec((nb, CHUNK, d), lambda b, c: (b, c, 0))
    return pl.pallas_call(
        _mlstm_kernel,
        grid=(bsz // nb, seq // CHUNK),
        in_specs=[act_spec, act_spec] + [_resident(a, 2) for a in consts],
        out_specs=act_spec,
        out_shape=jax.ShapeDtypeStruct((bsz, seq, d), BF16),
        scratch_shapes=[
            pltpu.VMEM((nb, CONV_TAIL + CHUNK, d), BF16),
            pltpu.VMEM((nb * M_HEADS, M_HEAD_DIM, M_HEAD_DIM), F32),
            pltpu.VMEM((nb * M_HEADS, 1, M_HEAD_DIM), F32),
            pltpu.VMEM((nb * M_HEADS, 1, 1), F32),
        ],
        compiler_params=_params("parallel", "arbitrary"),
        name="mlstm",
    )(m_in, m_z, *consts)


def _rope_tables(pos_row):
    t = pos_row.shape[1]
    freq = lax.broadcasted_iota(jnp.int32, (ROPE_HALF, 1), 0).astype(F32)
    inv_freq = jnp.exp(-math.log(ROPE_THETA) * freq / ROPE_HALF)
    ang = inv_freq * pos_row
    packed = jnp.concatenate([jnp.cos(ang), jnp.sin(ang), jnp.zeros((LANES - ROPE_DIM, t), F32)], axis=0)
    cs = packed.T
    roll = lambda shift: pltpu.roll(cs, shift, 1)
    lane = lax.broadcasted_iota(jnp.int32, (1, LANES), 1)
    sel = lambda lo, a, b: jnp.where((lane >= lo) & (lane < lo + ROPE_HALF), a, b)
    hd = A_HEAD_DIM
    cos_t = sel(0, cs, sel(ROPE_HALF, roll(ROPE_HALF), sel(hd, roll(hd), sel(hd + ROPE_HALF, roll(hd + ROPE_HALF), 1.0))))
    sin_t = sel(0, -roll(LANES - ROPE_HALF),
                sel(ROPE_HALF, cs, sel(hd, -roll(hd - ROPE_HALF), sel(hd + ROPE_HALF, roll(hd), 0.0))))
    return cos_t, sin_t


def _norm_rope(x, gain, mean_rot, tables):
    cos_t, sin_t = tables
    t = x.shape[0]
    n = x.shape[1] // LANES
    xs = jnp.concatenate([x[:, s * LANES:(s + 1) * LANES] for s in range(n)], axis=0)
    xg = xs * gain
    both = jnp.dot(jnp.concatenate([(xs * xs).astype(BF16), xg.astype(BF16)], axis=1), mean_rot,
                   preferred_element_type=F32)
    ms, partner = both[:, :LANES], both[:, LANES:]
    roped = xg.reshape(n, t, LANES) * cos_t + partner.reshape(n, t, LANES) * sin_t
    return roped.reshape(n * t, LANES) * lax.rsqrt(ms + EPS)


def _swa_kernel(q_ref, z_ref, kv_ref, pos_ref, bias_ref, mrot_ref, qg_ref, kg_ref, sink_ref, o_ref,
                k_buf, v_buf):
    t = CHUNK
    hd = A_HEAD_DIM
    blk = pl.program_id(1)
    n_seq = q_ref.shape[0]
    low = lax.broadcasted_iota(jnp.int32, (1, LANES), 1) < hd

    @pl.when(blk == 0)
    def _():
        ones = jnp.broadcast_to(jnp.where(low, 1.0, 0.0), (2 * t, LANES))
        for kv in range(n_seq * A_KV_HEADS):
            for part in range(2):
                k_buf[kv, 2 * part * t:(2 * part + 1) * t, :] = jnp.zeros((t, LANES), BF16)
                v_buf[kv, 2 * part * t:(2 * part + 1) * t, 0:LANES] = jnp.zeros((t, LANES), BF16)
                v_buf[kv, 2 * part * t:(2 * part + 2) * t, LANES:] = (ones if part == 0 else 1.0 - ones).astype(BF16)

    @pl.when(blk > 0)
    def _():
        for kv in range(n_seq * A_KV_HEADS):
            for part in range(2):
                k_buf[kv, 2 * part * t:(2 * part + 1) * t, :] = k_buf[kv, (2 * part + 1) * t:(2 * part + 2) * t, :]
                v_buf[kv, 2 * part * t:(2 * part + 1) * t, 0:LANES] = (
                    v_buf[kv, (2 * part + 1) * t:(2 * part + 2) * t, 0:LANES])

    mean_rot = mrot_ref[...]
    bias = bias_ref[...]
    upper = lax.broadcasted_iota(jnp.int32, (2 * t, 1), 0) < t
    for sq in range(n_seq):
        tables = _rope_tables(pos_ref[sq])
        base = sq * A_KV_HEADS
        kr = _norm_rope(kv_ref[sq, :, 0:A_KV].astype(F32), kg_ref[...], mean_rot, tables)
        for s in range(A_KV // LANES):
            for src, buf in ((kr[s * t:(s + 1) * t], k_buf),
                             (kv_ref[sq, :, A_KV + s * LANES:A_KV + (s + 1) * LANES].astype(F32), v_buf)):
                lo = jnp.where(low, src, 0.0)
                hi = jnp.where(low, 0.0, src)
                buf[base + 2 * s, t:2 * t, 0:LANES] = lo.astype(BF16)
                buf[base + 2 * s, 3 * t:4 * t, 0:LANES] = pltpu.roll(lo, hd, 1).astype(BF16)
                buf[base + 2 * s + 1, t:2 * t, 0:LANES] = pltpu.roll(hi, hd, 1).astype(BF16)
                buf[base + 2 * s + 1, 3 * t:4 * t, 0:LANES] = hi.astype(BF16)

        qr = _norm_rope(q_ref[sq].astype(F32), qg_ref[...], mean_rot, tables).astype(BF16)
        scores = [lax.dot_general(qr[2 * kv * t:(2 * kv + 2) * t], k_buf[base + kv], NT_DIMS,
                                  preferred_element_type=F32)
                  for kv in range(A_KV_HEADS)]
        outs = []
        for kv in range(A_KV_HEADS):
            sc = (scores[kv].reshape(2, t, 4 * t) + bias).reshape(2 * t, 4 * t)
            probs, sink_terms = [], []
            for half in range(2):
                sink = jnp.where(upper, sink_ref[4 * kv + half:4 * kv + half + 1, :],
                                 sink_ref[4 * kv + 2 + half:4 * kv + 3 + half, :])
                sc_h = [sc[:, (2 * half + c) * t:(2 * half + c + 1) * t] for c in range(2)]
                m = jnp.maximum(jnp.max(jnp.maximum(sc_h[0], sc_h[1]), axis=-1, keepdims=True), sink)
                probs += [jnp.exp2(sc_h[c] - m).astype(BF16) for c in range(2)]
                sink_terms.append(jnp.exp2(sink - m))
            o = jnp.dot(jnp.concatenate(probs, axis=1), v_buf[base + kv], preferred_element_type=F32)
            outs.append(o[:, :LANES] / (o[:, LANES:] + jnp.where(low, sink_terms[0], sink_terms[1])))
        for kv in range(A_KV_HEADS):
            for pair in range(2):
                qs = slice((2 * kv + pair) * LANES, (2 * kv + pair + 1) * LANES)
                o_ref[sq, :, qs] = (outs[kv][pair * t:(pair + 1) * t]
                                    * _silu(z_ref[sq, :, qs].astype(F32))).astype(o_ref.dtype)


def _swa(q, z, kv, pos, q_g, k_g, sinks):
    bsz, seq, d = q.shape
    t = CHUNK
    tile2 = lambda g: jnp.tile(g.astype(F32).reshape(1, A_HEAD_DIM), (1, LANES // A_HEAD_DIM))
    qi = jnp.arange(t)[:, None]
    ki = jnp.arange(t)[None, :]
    cur = jnp.where(ki <= qi, 0.0, NEG_BIG).astype(F32)
    prev = jnp.where(ki > qi, 0.0, NEG_BIG).astype(F32)
    first = jnp.concatenate([jnp.full((t, t), NEG_BIG, F32), cur], axis=1)
    later = jnp.concatenate([prev, cur], axis=1)
    bias = jnp.stack([jnp.tile(first, (1, 2)), jnp.tile(later, (1, 2))])
    lane = jnp.arange(LANES)
    in_head = lane % A_HEAD_DIM
    same_head = (lane[:, None] // A_HEAD_DIM == lane[None, :] // A_HEAD_DIM)
    partner = jnp.where(in_head < ROPE_HALF, lane + ROPE_HALF, jnp.where(in_head < ROPE_DIM, lane - ROPE_HALF, -1))
    zeros = jnp.zeros((LANES, LANES), F32)
    mean_rot = jnp.block([[same_head.astype(F32) / A_HEAD_DIM, zeros],
                          [zeros, (lane[:, None] == partner[None, :]).astype(F32)]]).astype(BF16)
    log2e = math.log2(math.e)
    sink_rows = jnp.broadcast_to(sinks.astype(F32).reshape(A_HEADS, 1) * log2e, (A_HEADS, LANES))
    consts = [mean_rot, tile2(q_g) * (A_HEAD_DIM ** -0.5 * log2e), tile2(k_g), sink_rows]
    nb = SEQS_PER_STEP
    act_spec = lambda width: pl.BlockSpec((nb, t, width), lambda b, c: (b, c, 0))
    return pl.pallas_call(
        _swa_kernel,
        grid=(bsz // nb, seq // t),
        in_specs=[act_spec(d), act_spec(d), act_spec(2 * A_KV),
                  pl.BlockSpec((nb, None, 1, t), lambda b, c: (b, c, 0, 0)),
                  pl.BlockSpec((None, t, 4 * t), lambda b, c: (jnp.minimum(c, 1), 0, 0))]
        + [_resident(a, 2) for a in consts],
        out_specs=act_spec(d),
        out_shape=jax.ShapeDtypeStruct((bsz, seq, d), BF16),
        scratch_shapes=[pltpu.VMEM((nb * A_KV_HEADS, 4 * t, LANES), BF16),
                        pltpu.VMEM((nb * A_KV_HEADS, 4 * t, 2 * LANES), BF16)],
        compiler_params=_params("parallel", "arbitrary"),
        name="swa",
    )(q, z, kv, pos, bias, *consts)


def kernel(x, c, positions, ada_w, ada_b, norm_g, ev_w_in, s5_a_re, s5_a_im, s5_log_dt, s5_b_re, s5_b_im, s5_c_re, s5_c_im, s5_d, s5_w_glu, m_conv_w, m_conv_b, m_wq, m_wk, m_wv, m_wo, m_w_if, m_b_if, m_norm_g, m_skip, ev_w_out, od_w_in, od_q_norm_g, od_k_norm_g, od_sinks, od_w_out):
    bsz, seq, d = x.shape
    assert d == D_MODEL and seq % CHUNK == 0 and bsz * S5_PERM_T == 2 * LANES
    mod = _adaln(c, ada_w, ada_b)
    h = x.astype(F32)
    pos = positions.astype(F32).reshape(bsz, seq // CHUNK, 1, CHUNK)
    a_q = A_HEADS * A_HEAD_DIM

    def opening(layer):
        i = layer // 2
        if layer % 2 == 0:
            return (layer, norm_g[layer], ev_w_in[i], [d] * 4)
        w = od_w_in[i]
        w = jnp.concatenate([w[:, :a_q], w[:, a_q + 2 * A_KV:], w[:, a_q:a_q + 2 * A_KV]], axis=1)
        return (layer, norm_g[layer], w, [a_q, a_q, 2 * A_KV])

    proj = _boundary(h, mod, open_=opening(0))
    for layer in range(DEPTH):
        i = layer // 2
        if layer % 2 == 0:
            s5_u, s5_z, m_in, m_z = proj
            tables = _s5_tables(s5_a_re[i], s5_a_im[i], s5_log_dt[i], s5_b_re[i], s5_b_im[i],
                                s5_c_re[i], s5_c_im[i])
            y_s5 = _s5(s5_u, s5_z, tables, s5_d[i], s5_w_glu[i])
            y_m = _mlstm(m_in, m_z, m_conv_w[i], m_conv_b[i], m_wq[i], m_wk[i], m_wv[i], m_wo[i],
                         m_w_if[i], m_b_if[i], m_norm_g[i], m_skip[i])
            closing = (layer, [y_s5, y_m], ev_w_out[i])
        else:
            q, z, kv = proj
            attn = _swa(q, z, kv, pos, od_q_norm_g[i], od_k_norm_g[i], od_sinks[i])
            closing = (layer, [attn], od_w_out[i])
        h, *proj = _boundary(h, mod, close=closing, open_=opening(layer + 1) if layer + 1 < DEPTH else None)
    return h.astype(x.dtype)
```

```python
import functools
import math

import jax
import jax.numpy as jnp
from jax import lax
from jax.experimental import pallas as pl
from jax.experimental.pallas import tpu as pltpu

F32 = jnp.float32
BF16 = jnp.bfloat16
HIGHEST = lax.Precision.HIGHEST

EPS = 1e-6
D_MODEL = 1024
DEPTH = 4
S5_GROUP = 16
S5_STATE = 64
S5_GROUPS = D_MODEL // S5_GROUP
S5_GROUPS_PER_TILE = 16
S5_TILES = S5_GROUPS // S5_GROUPS_PER_TILE
S5_TILE_CH = S5_GROUPS_PER_TILE * S5_GROUP
S5_TILE_ST = S5_GROUPS_PER_TILE * S5_STATE
S5_SCAN_COLS = 256
S5_TIME_TILE = 64
S5_PERM_T = 16
M_HEADS = 4
M_HEAD_DIM = D_MODEL // M_HEADS
M_CONV = 4
CHUNK = 128
CONV_TAIL = 16
A_HEAD_DIM = 64
A_HEADS = D_MODEL // A_HEAD_DIM
A_KV_HEADS = A_HEADS // 4
A_KV = A_KV_HEADS * A_HEAD_DIM
ROPE_THETA = 500000.0
ROPE_DIM = A_HEAD_DIM // 4
ROPE_HALF = ROPE_DIM // 2
LANES = 128
NEG_BIG = -1e30

ROW_TILE = 1024
SEQS_PER_STEP = 8
M_SEQS_PER_STEP = 8
V7X_VMEM_BYTES = 64 * 1024 * 1024
VMEM_LIMIT = V7X_VMEM_BYTES - 8 * 1024 * 1024
NT_DIMS = (((1,), (1,)), ((), ()))


_sigmoid = jax.nn.sigmoid


def _silu(x):
    return x * _sigmoid(x)


def _gelu_tanh(x):
    return 0.5 * x * (1.0 + jnp.tanh(math.sqrt(2.0 / math.pi) * (x + 0.044715 * (x * x * x))))


def _params(*sem):
    return pltpu.CompilerParams(dimension_semantics=sem, vmem_limit_bytes=VMEM_LIMIT)


def _resident(arr, n_grid):
    zeros = (0,) * arr.ndim
    index_map = (lambda i: zeros) if n_grid == 1 else (lambda i, j: zeros)
    return pl.BlockSpec(arr.shape, index_map, pipeline_mode=pl.Buffered(1))


def _adaln_kernel(c_ref, w_ref, b_ref, o_ref):
    s = _silu(c_ref[...])
    o_ref[...] = jnp.dot(s, w_ref[...], preferred_element_type=F32, precision=HIGHEST) + b_ref[...]


def _adaln(c, ada_w, ada_b):
    bsz = c.shape[0]
    mod = pl.pallas_call(
        _adaln_kernel,
        grid=(DEPTH, 3),
        in_specs=[
            pl.BlockSpec((bsz, D_MODEL), lambda l, j: (0, 0)),
            pl.BlockSpec((None, D_MODEL, D_MODEL), lambda l, j: (l, 0, j)),
            pl.BlockSpec((None, None, 1, D_MODEL), lambda l, j: (l, j, 0, 0)),
        ],
        out_specs=pl.BlockSpec((None, None, bsz, D_MODEL), lambda l, j: (l, j, 0, 0)),
        out_shape=jax.ShapeDtypeStruct((DEPTH, 3, bsz, D_MODEL), F32),
        compiler_params=_params("parallel", "parallel"),
        name="adaln",
    )(c.astype(F32), ada_w.astype(F32), ada_b.astype(F32).reshape(DEPTH, 3, 1, D_MODEL))
    return jnp.transpose(mod, (0, 2, 1, 3))


def _mod_spec(layer):
    return pl.BlockSpec((None, None, 3, D_MODEL), lambda b, i: (layer, b, 0, 0))


def _boundary_kernel(*refs, n_parts, n_outs):
    refs = list(refs)
    h_ref = refs.pop(0)
    if n_parts:
        mod_out_ref = refs.pop(0)
        y_refs = [refs.pop(0) for _ in range(n_parts)]
        w_out_ref = refs.pop(0)
    if n_outs:
        mod_in_ref, g_ref, w_in_ref = refs.pop(0), refs.pop(0), refs.pop(0)
    h = h_ref[...]
    if n_parts:
        acc = None
        for i, y_ref in enumerate(y_refs):
            k = y_ref.shape[-1]
            part = jnp.dot(y_ref[...], w_out_ref[i * k:(i + 1) * k, :], preferred_element_type=F32)
            acc = part if acc is None else acc + part
        h = h + mod_out_ref[2:3, :] * acc
        refs.pop(0)[...] = h
    if n_outs:
        ms = jnp.mean(h * h, axis=-1, keepdims=True)
        hn = h * lax.rsqrt(ms + EPS) * g_ref[...] * (1.0 + mod_in_ref[1:2, :]) + mod_in_ref[0:1, :]
        hb = hn.astype(BF16)
        c0 = 0
        for o_ref in refs:
            width = o_ref.shape[-1]
            o_ref[...] = jnp.dot(hb, w_in_ref[:, c0:c0 + width], preferred_element_type=F32).astype(o_ref.dtype)
            c0 += width


def _boundary(h, mod, close=None, open_=None):
    bsz, seq, d = h.shape
    rt = min(ROW_TILE, seq)
    act_spec = lambda width: pl.BlockSpec((None, rt, width), lambda b, i: (b, i, 0))
    args, in_specs, out_specs, out_shape = [h], [act_spec(d)], [], []
    n_parts = n_outs = 0
    if close is not None:
        layer, parts, w_out = close
        n_parts = len(parts)
        w_out = w_out.astype(BF16)
        args += [mod, *parts, w_out]
        in_specs += [_mod_spec(layer)] + [act_spec(p.shape[-1]) for p in parts] + [_resident(w_out, 2)]
        out_specs.append(act_spec(d))
        out_shape.append(jax.ShapeDtypeStruct((bsz, seq, d), F32))
    if open_ is not None:
        layer, gain, w_in, widths = open_
        assert sum(widths) == w_in.shape[1]
        n_outs = len(widths)
        w_in = w_in.astype(BF16)
        args += [mod, gain.astype(F32).reshape(1, d), w_in]
        in_specs += [_mod_spec(layer), pl.BlockSpec((1, d), lambda b, i: (0, 0)), _resident(w_in, 2)]
        out_specs += [act_spec(wd) for wd in widths]
        out_shape += [jax.ShapeDtypeStruct((bsz, seq, wd), BF16) for wd in widths]
    return pl.pallas_call(
        functools.partial(_boundary_kernel, n_parts=n_parts, n_outs=n_outs),
        grid=(bsz, seq // rt),
        in_specs=in_specs,
        out_specs=out_specs,
        out_shape=out_shape,
        compiler_params=_params("parallel", "parallel"),
        name="boundary",
    )(*args)


def _s5_kernel(u_ref, z_ref, perm_ref, permt_ref, bmat_ref, cmat_ref, are_ref, aim_ref, d_ref, wglu_ref,
               o_ref, st_re, st_im, u_tm, bu_re, bu_im, x_re, x_im, y_acc):
    bsz, tt, d = u_ref.shape
    rows = tt * bsz
    pt = S5_PERM_T
    blk_rows = pt * bsz

    @pl.when(pl.program_id(0) == 0)
    def _():
        st_re[...] = jnp.zeros_like(st_re)
        st_im[...] = jnp.zeros_like(st_im)

    for tb in range(tt // pt):
        by_seq = jnp.concatenate([u_ref[b, tb * pt:(tb + 1) * pt, :] for b in range(bsz)], axis=0)
        u_tm[tb * blk_rows:(tb + 1) * blk_rows, :] = jnp.dot(
            perm_ref[...], by_seq, preferred_element_type=F32).astype(BF16)


    def project_in(j):
        u_j = u_tm[:, j * S5_TILE_CH:(j + 1) * S5_TILE_CH]
        bu_re[j % 2] = jnp.dot(u_j, bmat_ref[j, :, :S5_TILE_ST], preferred_element_type=F32)
        bu_im[j % 2] = jnp.dot(u_j, bmat_ref[j, :, S5_TILE_ST:], preferred_element_type=F32)

    def recur(j):
        for c0 in range(0, S5_TILE_ST, S5_SCAN_COLS):
            cols = slice(c0, c0 + S5_SCAN_COLS)
            st_cols = slice(j * S5_TILE_ST + c0, j * S5_TILE_ST + c0 + S5_SCAN_COLS)
            a_re = jnp.broadcast_to(are_ref[j, :, cols], (bsz, S5_SCAN_COLS))
            a_im = jnp.broadcast_to(aim_ref[j, :, cols], (bsz, S5_SCAN_COLS))
            s_re, s_im = st_re[:, st_cols], st_im[:, st_cols]
            for t in range(tt):
                r = slice(t * bsz, (t + 1) * bsz)
                s_re, s_im = (a_re * s_re - a_im * s_im + bu_re[j % 2, r, cols],
                              a_re * s_im + a_im * s_re + bu_im[j % 2, r, cols])
                x_re[j % 2, r, cols] = s_re.astype(BF16)
                x_im[j % 2, r, cols] = s_im.astype(BF16)
            st_re[:, st_cols] = s_re
            st_im[:, st_cols] = s_im

    def project_out(j):
        ch = slice(j * S5_TILE_CH, (j + 1) * S5_TILE_CH)
        y_j = (jnp.dot(x_re[j % 2], cmat_ref[j, :S5_TILE_ST, :], preferred_element_type=F32)
               + jnp.dot(x_im[j % 2], cmat_ref[j, S5_TILE_ST:, :], preferred_element_type=F32))
        y_acc[:, ch] = _gelu_tanh(y_j + d_ref[:, ch] * u_tm[:, ch].astype(F32))

    project_in(0)
    for j in range(S5_TILES):
        if j + 1 < S5_TILES:
            project_in(j + 1)
        recur(j)
        project_out(j)

    def glu_lin(k):
        return jnp.dot(y_acc[k * blk_rows:(k + 1) * blk_rows, :].astype(BF16), wglu_ref[...],
                       preferred_element_type=F32)

    n_blk = tt // pt
    lin = {0: glu_lin(0)}
    for k in range(n_blk):
        if k + 1 < n_blk:
            lin[k + 1] = glu_lin(k + 1)
        r = slice(k * blk_rows, (k + 1) * blk_rows)
        glu = (y_acc[r, :] * _sigmoid(lin.pop(k))).astype(BF16)
        by_seq = jnp.dot(permt_ref[...], glu, preferred_element_type=F32)
        for b in range(bsz):
            gate = _silu(z_ref[b, k * pt:(k + 1) * pt, :].astype(F32))
            o_ref[b, k * pt:(k + 1) * pt, :] = (by_seq[b * pt:(b + 1) * pt, :] * gate).astype(o_ref.dtype)


def _s5_discretise_kernel(are_ref, aim_ref, logdt_ref, bre_ref, bim_ref, expand_ref,
                          abr_ref, abi_ref, bbr_ref, bbi_ref):
    a_re, a_im = are_ref[...], aim_ref[...]
    dt = jnp.exp(logdt_ref[...])
    mag = jnp.exp(a_re * dt)
    ab_re, ab_im = mag * jnp.cos(a_im * dt), mag * jnp.sin(a_im * dt)
    abr_ref[...] = ab_re
    abi_ref[...] = ab_im
    n_re, n_im = ab_re - 1.0, ab_im
    inv = 1.0 / (a_re * a_re + a_im * a_im)
    f_re = (n_re * a_re + n_im * a_im) * inv
    f_im = (n_im * a_re - n_re * a_im) * inv
    spread = lambda v: jnp.dot(v, expand_ref[...], preferred_element_type=F32, precision=HIGHEST)
    f_re, f_im = spread(f_re), spread(f_im)
    b_re, b_im = bre_ref[...], bim_ref[...]
    bbr_ref[...] = f_re * b_re - f_im * b_im
    bbi_ref[...] = f_re * b_im + f_im * b_re


def _s5_tables(a_re, a_im, log_dt, b_re, b_im, c_re, c_im):
    g, n, p = S5_GROUPS, S5_STATE, S5_GROUP
    gt, nt = S5_GROUPS_PER_TILE, S5_TILES
    expand = jnp.repeat(jnp.eye(n, dtype=F32), p, axis=1)
    ab_re, ab_im, bb_re, bb_im = pl.pallas_call(
        _s5_discretise_kernel,
        out_shape=[jax.ShapeDtypeStruct((g, n), F32)] * 2 + [jax.ShapeDtypeStruct((g, n * p), F32)] * 2,
        name="s5_discretise",
    )(a_re.astype(F32), a_im.astype(F32), log_dt.astype(F32).reshape(g, 1),
      b_re.astype(F32).reshape(g, n * p), b_im.astype(F32).reshape(g, n * p), expand)
    eye = jnp.eye(gt, dtype=F32)

    def in_block(b):
        return jnp.einsum('jgnp,gh->jgphn', b.reshape(nt, gt, n, p), eye).reshape(nt, gt * p, gt * n)

    def out_block(c):
        return jnp.einsum('jgpn,gh->jgnhp', c.reshape(nt, gt, p, n), eye).reshape(nt, gt * n, gt * p)

    bmat = jnp.concatenate([in_block(bb_re), in_block(bb_im)], axis=-1)
    cmat = jnp.concatenate([out_block(c_re.astype(F32)), out_block(-c_im.astype(F32))], axis=1)
    a_cols = lambda v: v.reshape(nt, 1, gt * n)
    return bmat.astype(BF16), cmat.astype(BF16), a_cols(ab_re), a_cols(ab_im)


def _s5(u, z, tables, d_skip, w_glu):
    bsz, seq, d = u.shape
    tt = min(S5_TIME_TILE, seq)
    rows = tt * bsz
    bmat, cmat, a_re, a_im = tables
    blk_rows = S5_PERM_T * bsz
    perm = jnp.eye(blk_rows, dtype=BF16).reshape(bsz, S5_PERM_T, blk_rows).transpose(1, 0, 2).reshape(blk_rows, blk_rows)
    consts = [perm, perm.T, bmat, cmat, a_re, a_im, d_skip.astype(F32).reshape(1, d), w_glu.astype(BF16)]
    act_spec = pl.BlockSpec((bsz, tt, d), lambda i: (0, i, 0))
    return pl.pallas_call(
        _s5_kernel,
        grid=(seq // tt,),
        in_specs=[act_spec, act_spec] + [_resident(a, 1) for a in consts],
        out_specs=act_spec,
        out_shape=jax.ShapeDtypeStruct((bsz, seq, d), BF16),
        scratch_shapes=[
            pltpu.VMEM((bsz, S5_GROUPS * S5_STATE), F32),
            pltpu.VMEM((bsz, S5_GROUPS * S5_STATE), F32),
            pltpu.VMEM((rows, d), BF16),
            pltpu.VMEM((2, rows, S5_TILE_ST), F32),
            pltpu.VMEM((2, rows, S5_TILE_ST), F32),
            pltpu.VMEM((2, rows, S5_TILE_ST), BF16),
            pltpu.VMEM((2, rows, S5_TILE_ST), BF16),
            pltpu.VMEM((rows, d), F32),
        ],
        compiler_params=_params("arbitrary"),
        name="s5",
    )(u, z, *consts)


def _mlstm_kernel(mi_ref, mz_ref, shift_ref, convw_ref, convb_ref, wq_ref, wkt_ref, wvo_ref, wif_ref, bif_ref,
                  ng_ref, skip_ref, o_ref, ext, c_st, n_st, m_st):
    t = CHUNK
    dh = M_HEAD_DIM
    seqs = range(mi_ref.shape[0])
    heads = range(M_HEADS)
    hsl = [slice(h * dh, (h + 1) * dh) for h in heads]
    slot = lambda sq, h: sq * M_HEADS + h

    @pl.when(pl.program_id(1) == 0)
    def _():
        for sq in seqs:
            ext[sq, 0:CONV_TAIL, :] = jnp.zeros((CONV_TAIL, D_MODEL), BF16)
        c_st[...] = jnp.zeros_like(c_st)
        n_st[...] = jnp.zeros_like(n_st)
        m_st[...] = jnp.zeros_like(m_st)

    tri = lax.broadcasted_iota(jnp.int32, (t, t), 0) >= lax.broadcasted_iota(jnp.int32, (t, t), 1)
    tri_b = tri.astype(BF16)

    def project(sq):
        f = {}
        mi_b = mi_ref[sq]
        ext[sq, CONV_TAIL:, :] = mi_b
        taps = jnp.dot(shift_ref[...], ext[sq], preferred_element_type=F32)
        ext[sq, 0:CONV_TAIL, :] = mi_b[t - CONV_TAIL:, :]
        f["vo"] = [jnp.dot(mi_b[:, hsl[h]], wvo_ref[h], preferred_element_type=F32) for h in heads]
        conv = convb_ref[...] + mi_b.astype(F32) * convw_ref[M_CONV - 1:M_CONV, :]
        for k in range(M_CONV - 1):
            conv = conv + taps[k * t:(k + 1) * t, :] * convw_ref[k:k + 1, :]
        f["xc"] = _silu(conv)
        xc_b = f["xc"].astype(BF16)
        f["gates"] = (jnp.dot(xc_b, wif_ref[0:D_MODEL, :], preferred_element_type=F32)
                      + jnp.dot(mi_b, wif_ref[D_MODEL:, :], preferred_element_type=F32) + bif_ref[...])
        f["q"] = [jnp.dot(xc_b[:, hsl[h]], wq_ref[h], preferred_element_type=F32) for h in heads]
        f["kt"] = [lax.dot_general(wkt_ref[h], xc_b[:, hsl[h]], NT_DIMS, preferred_element_type=F32)
                   for h in heads]
        rest = jax.nn.log_sigmoid(f["gates"])
        f["cum"] = None
        for _ in range(3):
            term = rest.astype(BF16)
            part = jnp.dot(tri_b, term, preferred_element_type=F32)
            f["cum"] = part if f["cum"] is None else f["cum"] + part
            rest = rest - term.astype(F32)
        q_b = [f["q"][h].astype(BF16) for h in heads]
        f["kt_b"] = [f["kt"][h].astype(BF16) for h in heads]
        f["c_prev"] = [c_st[slot(sq, h)] for h in heads]
        f["qkt"] = [jnp.dot(q_b[h], f["kt_b"][h], preferred_element_type=F32) for h in heads]
        f["qc"] = [jnp.dot(q_b[h], f["c_prev"][h].astype(BF16), preferred_element_type=F32) for h in heads]
        f["g_row"], f["b_row"] = f["gates"].T, f["cum"].T
        return f

    def cell(sq, f):
        st = [slot(sq, h) for h in heads]
        n_prev = [n_st[st[h]] for h in heads]
        m_prev = [m_st[st[h]] for h in heads]
        v_b = [f["vo"][h][:, :dh].astype(BF16) for h in heads]
        s, g, m_t, w_row, kwt, decay = [], [], [], [], [], []
        for h in heads:
            bc = f["cum"][:, M_HEADS + h:M_HEADS + h + 1]
            br = f["b_row"][M_HEADS + h:M_HEADS + h + 1, :]
            li = f["g_row"][h:h + 1, :]
            d_log = jnp.where(tri, bc - br + li, -jnp.inf)
            inter = bc + m_prev[h]
            m_t.append(jnp.maximum(inter, jnp.max(d_log, axis=-1, keepdims=True)))
            s.append(f["qkt"][h] * jnp.exp(d_log - m_t[h]))
            g.append(jnp.exp(inter - m_t[h]))
            b_last = br[:, t - 1:t]
            w_log = b_last - br + li
            m_new = jnp.maximum(b_last + m_prev[h], jnp.max(w_log, axis=-1, keepdims=True))
            w_row.append(jnp.exp(w_log - m_new))
            kwt.append((f["kt"][h] * w_row[h]).astype(BF16))
            decay.append(jnp.exp(b_last + m_prev[h] - m_new))
            m_st[st[h]] = m_new
        sv = [jnp.dot(s[h].astype(BF16), v_b[h], preferred_element_type=F32) for h in heads]
        kv = [jnp.dot(kwt[h], v_b[h], preferred_element_type=F32) for h in heads]
        ksum = [lax.dot_general(w_row[h].astype(BF16), f["kt_b"][h], NT_DIMS, preferred_element_type=F32)
                for h in heads]
        for h in heads:
            c_st[st[h]] = decay[h] * f["c_prev"][h] + kv[h]
            n_st[st[h]] = decay[h] * n_prev[h] + ksum[h]
            num = sv[h] + g[h] * f["qc"][h]
            den = (jnp.sum(s[h], axis=-1, keepdims=True)
                   + g[h] * jnp.sum(f["q"][h] * n_prev[h], axis=-1, keepdims=True))
            inv = 1.0 / jnp.maximum(jnp.abs(den), jnp.exp(-m_t[h]))
            oh = _sigmoid(f["vo"][h][:, dh:]) * (num * inv)
            ms = jnp.mean(oh * oh, axis=-1, keepdims=True)
            y = oh * lax.rsqrt(ms + EPS) * ng_ref[:, hsl[h]] + skip_ref[:, hsl[h]] * f["xc"][:, hsl[h]]
            o_ref[sq, :, hsl[h]] = (y * _silu(mz_ref[sq, :, hsl[h]].astype(F32))).astype(o_ref.dtype)

    fronts = [project(sq) for sq in seqs]
    for sq in seqs:
        cell(sq, fronts[sq])


def _mlstm(m_in, m_z, conv_w, conv_b, wq, wk, wv, wo, w_if, b_if, norm_g, skip):
    bsz, seq, d = m_in.shape
    t = CHUNK
    wkt = jnp.swapaxes(wk * (M_HEAD_DIM ** -0.5), 1, 2).astype(BF16)
    wvo = jnp.concatenate([wv, wo], axis=-1).astype(BF16)
    pad = LANES - 2 * M_HEADS
    w_if = jnp.pad(w_if.astype(BF16), ((0, 0), (0, pad)))
    b_if = jnp.pad(b_if.astype(F32), (0, pad)).reshape(1, LANES)
    rows = jnp.arange((M_CONV - 1) * t)
    src = CONV_TAIL + rows % t + rows // t - (M_CONV - 1)
    shift = (src[:, None] == jnp.arange(CONV_TAIL + t)[None, :]).astype(BF16)
    consts = [shift, conv_w.astype(F32), conv_b.astype(F32).reshape(1, d), wq.astype(BF16), wkt, wvo, w_if, b_if,
              norm_g.astype(F32).reshape(1, d), skip.astype(F32).reshape(1, d)]
    nb = M_SEQS_PER_STEP
    act_spec = pl.BlockSpec((nb, CHUNK, d), lambda b, c: (b, c, 0))
    return pl.pallas_call(
        _mlstm_kernel,
        grid=(bsz // nb, seq // CHUNK),
        in_specs=[act_spec, act_spec] + [_resident(a, 2) for a in consts],
        out_specs=act_spec,
        out_shape=jax.ShapeDtypeStruct((bsz, seq, d), BF16),
        scratch_shapes=[
            pltpu.VMEM((nb, CONV_TAIL + CHUNK, d), BF16),
            pltpu.VMEM((nb * M_HEADS, M_HEAD_DIM, M_HEAD_DIM), F32),
            pltpu.VMEM((nb * M_HEADS, 1, M_HEAD_DIM), F32),
            pltpu.VMEM((nb * M_HEADS, 1, 1), F32),
        ],
        compiler_params=_params("parallel", "arbitrary"),
        name="mlstm",
    )(m_in, m_z, *consts)


def _rope_tables(pos_row):
    t = pos_row.shape[1]
    freq = lax.broadcasted_iota(jnp.int32, (ROPE_HALF, 1), 0).astype(F32)
    inv_freq = jnp.exp(-math.log(ROPE_THETA) * freq / ROPE_HALF)
    ang = inv_freq * pos_row
    packed = jnp.concatenate([jnp.cos(ang), jnp.sin(ang), jnp.zeros((LANES - ROPE_DIM, t), F32)], axis=0)
    cs = packed.T
    roll = lambda shift: pltpu.roll(cs, shift, 1)
    lane = lax.broadcasted_iota(jnp.int32, (1, LANES), 1)
    sel = lambda lo, a, b: jnp.where((lane >= lo) & (lane < lo + ROPE_HALF), a, b)
    hd = A_HEAD_DIM
    cos_t = sel(0, cs, sel(ROPE_HALF, roll(ROPE_HALF), sel(hd, roll(hd), sel(hd + ROPE_HALF, roll(hd + ROPE_HALF), 1.0))))
    sin_t = sel(0, -roll(LANES - ROPE_HALF),
                sel(ROPE_HALF, cs, sel(hd, -roll(hd - ROPE_HALF), sel(hd + ROPE_HALF, roll(hd), 0.0))))
    return cos_t, sin_t


def _norm_rope(x, gain, mean_rot, tables):
    cos_t, sin_t = tables
    t = x.shape[0]
    n = x.shape[1] // LANES
    xs = jnp.concatenate([x[:, s * LANES:(s + 1) * LANES] for s in range(n)], axis=0)
    xg = xs * gain
    both = jnp.dot(jnp.concatenate([(xs * xs).astype(BF16), xg.astype(BF16)], axis=1), mean_rot,
                   preferred_element_type=F32)
    ms, partner = both[:, :LANES], both[:, LANES:]
    roped = xg.reshape(n, t, LANES) * cos_t + partner.reshape(n, t, LANES) * sin_t
    return roped.reshape(n * t, LANES) * lax.rsqrt(ms + EPS)


def _swa_kernel(q_ref, z_ref, kv_ref, pos_ref, bias_ref, mrot_ref, qg_ref, kg_ref, sink_ref, o_ref,
                k_buf, v_buf):
    t = CHUNK
    hd = A_HEAD_DIM
    blk = pl.program_id(1)
    n_seq = q_ref.shape[0]
    low = lax.broadcasted_iota(jnp.int32, (1, LANES), 1) < hd

    @pl.when(blk == 0)
    def _():
        ones = jnp.broadcast_to(jnp.where(low, 1.0, 0.0), (2 * t, LANES))
        for kv in range(n_seq * A_KV_HEADS):
            for part in range(2):
                k_buf[kv, 2 * part * t:(2 * part + 1) * t, :] = jnp.zeros((t, LANES), BF16)
                v_buf[kv, 2 * part * t:(2 * part + 1) * t, 0:LANES] = jnp.zeros((t, LANES), BF16)
                v_buf[kv, 2 * part * t:(2 * part + 2) * t, LANES:] = (ones if part == 0 else 1.0 - ones).astype(BF16)

    @pl.when(blk > 0)
    def _():
        for kv in range(n_seq * A_KV_HEADS):
            for part in range(2):
                k_buf[kv, 2 * part * t:(2 * part + 1) * t, :] = k_buf[kv, (2 * part + 1) * t:(2 * part + 2) * t, :]
                v_buf[kv, 2 * part * t:(2 * part + 1) * t, 0:LANES] = (
                    v_buf[kv, (2 * part + 1) * t:(2 * part + 2) * t, 0:LANES])

    mean_rot = mrot_ref[...]
    bias = bias_ref[...]
    upper = lax.broadcasted_iota(jnp.int32, (2 * t, 1), 0) < t
    for sq in range(n_seq):
        tables = _rope_tables(pos_ref[sq])
        base = sq * A_KV_HEADS
        kr = _norm_rope(kv_ref[sq, :, 0:A_KV].astype(F32), kg_ref[...], mean_rot, tables)
        for s in range(A_KV // LANES):
            for src, buf in ((kr[s * t:(s + 1) * t], k_buf),
                             (kv_ref[sq, :, A_KV + s * LANES:A_KV + (s + 1) * LANES].astype(F32), v_buf)):
                lo = jnp.where(low, src, 0.0)
                hi = jnp.where(low, 0.0, src)
                buf[base + 2 * s, t:2 * t, 0:LANES] = lo.astype(BF16)
                buf[base + 2 * s, 3 * t:4 * t, 0:LANES] = pltpu.roll(lo, hd, 1).astype(BF16)
                buf[base + 2 * s + 1, t:2 * t, 0:LANES] = pltpu.roll(hi, hd, 1).astype(BF16)
                buf[base + 2 * s + 1, 3 * t:4 * t, 0:LANES] = hi.astype(BF16)

        qr = _norm_rope(q_ref[sq].astype(F32), qg_ref[...], mean_rot, tables).astype(BF16)
        scores = [lax.dot_general(qr[2 * kv * t:(2 * kv + 2) * t], k_buf[base + kv], NT_DIMS,
                                  preferred_element_type=F32)
                  for kv in range(A_KV_HEADS)]
        outs = []
        for kv in range(A_KV_HEADS):
            sc = (scores[kv].reshape(2, t, 4 * t) + bias).reshape(2 * t, 4 * t)
            probs, sink_terms = [], []
            for half in range(2):
                sink = jnp.where(upper, sink_ref[4 * kv + half:4 * kv + half + 1, :],
                                 sink_ref[4 * kv + 2 + half:4 * kv + 3 + half, :])
                sc_h = [sc[:, (2 * half + c) * t:(2 * half + c + 1) * t] for c in range(2)]
                m = jnp.maximum(jnp.max(jnp.maximum(sc_h[0], sc_h[1]), axis=-1, keepdims=True), sink)
                probs += [jnp.exp2(sc_h[c] - m).astype(BF16) for c in range(2)]
                sink_terms.append(jnp.exp2(sink - m))
            o = jnp.dot(jnp.concatenate(probs, axis=1), v_buf[base + kv], preferred_element_type=F32)
            outs.append(o[:, :LANES] / (o[:, LANES:] + jnp.where(low, sink_terms[0], sink_terms[1])))
        for kv in range(A_KV_HEADS):
            for pair in range(2):
                qs = slice((2 * kv + pair) * LANES, (2 * kv + pair + 1) * LANES)
                o_ref[sq, :, qs] = (outs[kv][pair * t:(pair + 1) * t]
                                    * _silu(z_ref[sq, :, qs].astype(F32))).astype(o_ref.dtype)


def _swa(q, z, kv, pos, q_g, k_g, sinks):
    bsz, seq, d = q.shape
    t = CHUNK
    tile2 = lambda g: jnp.tile(g.astype(F32).reshape(1, A_HEAD_DIM), (1, LANES // A_HEAD_DIM))
    qi = jnp.arange(t)[:, None]
    ki = jnp.arange(t)[None, :]
    cur = jnp.where(ki <= qi, 0.0, NEG_BIG).astype(F32)
    prev = jnp.where(ki > qi, 0.0, NEG_BIG).astype(F32)
    first = jnp.concatenate([jnp.full((t, t), NEG_BIG, F32), cur], axis=1)
    later = jnp.concatenate([prev, cur], axis=1)
    bias = jnp.stack([jnp.tile(first, (1, 2)), jnp.tile(later, (1, 2))])
    lane = jnp.arange(LANES)
    in_head = lane % A_HEAD_DIM
    same_head = (lane[:, None] // A_HEAD_DIM == lane[None, :] // A_HEAD_DIM)
    partner = jnp.where(in_head < ROPE_HALF, lane + ROPE_HALF, jnp.where(in_head < ROPE_DIM, lane - ROPE_HALF, -1))
    zeros = jnp.zeros((LANES, LANES), F32)
    mean_rot = jnp.block([[same_head.astype(F32) / A_HEAD_DIM, zeros],
                          [zeros, (lane[:, None] == partner[None, :]).astype(F32)]]).astype(BF16)
    log2e = math.log2(math.e)
    sink_rows = jnp.broadcast_to(sinks.astype(F32).reshape(A_HEADS, 1) * log2e, (A_HEADS, LANES))
    consts = [mean_rot, tile2(q_g) * (A_HEAD_DIM ** -0.5 * log2e), tile2(k_g), sink_rows]
    nb = SEQS_PER_STEP
    act_spec = lambda width: pl.BlockSpec((nb, t, width), lambda b, c: (b, c, 0))
    return pl.pallas_call(
        _swa_kernel,
        grid=(bsz // nb, seq // t),
        in_specs=[act_spec(d), act_spec(d), act_spec(2 * A_KV),
                  pl.BlockSpec((nb, None, 1, t), lambda b, c: (b, c, 0, 0)),
                  pl.BlockSpec((None, t, 4 * t), lambda b, c: (jnp.minimum(c, 1), 0, 0))]
        + [_resident(a, 2) for a in consts],
        out_specs=act_spec(d),
        out_shape=jax.ShapeDtypeStruct((bsz, seq, d), BF16),
        scratch_shapes=[pltpu.VMEM((nb * A_KV_HEADS, 4 * t, LANES), BF16),
                        pltpu.VMEM((nb * A_KV_HEADS, 4 * t, 2 * LANES), BF16)],
        compiler_params=_params("parallel", "arbitrary"),
        name="swa",
    )(q, z, kv, pos, bias, *consts)


def kernel(x, c, positions, ada_w, ada_b, norm_g, ev_w_in, s5_a_re, s5_a_im, s5_log_dt, s5_b_re, s5_b_im, s5_c_re, s5_c_im, s5_d, s5_w_glu, m_conv_w, m_conv_b, m_wq, m_wk, m_wv, m_wo, m_w_if, m_b_if, m_norm_g, m_skip, ev_w_out, od_w_in, od_q_norm_g, od_k_norm_g, od_sinks, od_w_out):
    bsz, seq, d = x.shape
    assert d == D_MODEL and seq % CHUNK == 0 and bsz * S5_PERM_T == 2 * LANES
    mod = _adaln(c, ada_w, ada_b)
    h = x.astype(F32)
    pos = positions.astype(F32).reshape(bsz, seq // CHUNK, 1, CHUNK)
    a_q = A_HEADS * A_HEAD_DIM

    def opening(layer):
        i = layer // 2
        if layer % 2 == 0:
            return (layer, norm_g[layer], ev_w_in[i], [d] * 4)
        w = od_w_in[i]
        w = jnp.concatenate([w[:, :a_q], w[:, a_q + 2 * A_KV:], w[:, a_q:a_q + 2 * A_KV]], axis=1)
        return (layer, norm_g[layer], w, [a_q, a_q, 2 * A_KV])

    proj = _boundary(h, mod, open_=opening(0))
    for layer in range(DEPTH):
        i = layer // 2
        if layer % 2 == 0:
            s5_u, s5_z, m_in, m_z = proj
            tables = _s5_tables(s5_a_re[i], s5_a_im[i], s5_log_dt[i], s5_b_re[i], s5_b_im[i],
                                s5_c_re[i], s5_c_im[i])
            y_s5 = _s5(s5_u, s5_z, tables, s5_d[i], s5_w_glu[i])
            y_m = _mlstm(m_in, m_z, m_conv_w[i], m_conv_b[i], m_wq[i], m_wk[i], m_wv[i], m_wo[i],
                         m_w_if[i], m_b_if[i], m_norm_g[i], m_skip[i])
            closing = (layer, [y_s5, y_m], ev_w_out[i])
        else:
            q, z, kv = proj
            attn = _swa(q, z, kv, pos, od_q_norm_g[i], od_k_norm_g[i], od_sinks[i])
            closing = (layer, [attn], od_w_out[i])
        h, *proj = _boundary(h, mod, close=closing, open_=opening(layer + 1) if layer + 1 < DEPTH else None)
    return h.astype(x.dtype)
```
